```python
import jax, jax.numpy as jnp
from jax import lax
import numpy as np

D_MODEL = 1024
BATCH = 8
SEQ = 4096
DEPTH = 1

N_SUBLAYERS = 3
N_MOD = 3
D_FF = 2816
FFN_RES = 0.5
MLA_HEADS = 8
MLA_Q_LORA = 256
MLA_KV_LORA = 128
MLA_NOPE = 64
MLA_ROPE = 32
MLA_V = 64
MLA_THETA = 10000.0
Q_BLOCK = 128
DIL_PATTERNS = ((128, 1), (512, 4), (2048, 16))
N_DIL_GROUPS = 3
DIL_HEADS_PER_GROUP = 4
DIL_HEAD_DIM = 64
ROPE_THETA = 500000.0
ROPE_PART = DIL_HEAD_DIM // 4
DIL_QKV = 3 * N_DIL_GROUPS * DIL_HEADS_PER_GROUP * DIL_HEAD_DIM
IN_SPLITS = (MLA_Q_LORA, MLA_KV_LORA, MLA_ROPE, DIL_QKV, D_MODEL, D_MODEL)
D_IN = MLA_Q_LORA + MLA_KV_LORA + MLA_ROPE + DIL_QKV + 2 * D_MODEL
NORM_EPS = 1e-6
NEG_INF = -1e30

kernel_name = 'hybrid_mla_dilated_macaron_block'


def rms_norm(x, g):
    xf = x.astype(jnp.float32)
    y = xf * lax.rsqrt(jnp.mean(xf * xf, axis=-1, keepdims=True) + NORM_EPS)
    return (y * g.astype(jnp.float32)).astype(x.dtype)


def modulate(x, shift, scale):
    return x * (1 + scale[:, None, :]) + shift[:, None, :]


def rope_cos_sin(positions, dim, theta):
    inv = theta ** (-jnp.arange(0, dim, 2, dtype=jnp.float32) / dim)
    ang = positions.astype(jnp.float32)[..., None] * inv
    return jnp.cos(ang), jnp.sin(ang)


def apply_rope(t, cos, sin):
    half = t.shape[-1] // 2
    t1 = t[..., :half].astype(jnp.float32)
    t2 = t[..., half:].astype(jnp.float32)
    return jnp.concatenate([t1 * cos - t2 * sin, t2 * cos + t1 * sin], axis=-1).astype(t.dtype)


def partial_rope(t, cos, sin):
    return jnp.concatenate([apply_rope(t[..., :ROPE_PART], cos, sin), t[..., ROPE_PART:]], axis=-1)


def swiglu(h, w_gate, w_up, w_down):
    return (jax.nn.silu(h @ w_gate) * (h @ w_up)) @ w_down


def mla_attention(q_nope, q_rope, k_nope, k_rope, v):
    B, S, H, _ = q_nope.shape
    nblk = S // Q_BLOCK
    scale = (MLA_NOPE + MLA_ROPE) ** -0.5

    def to_blocks(t):
        return jnp.moveaxis(t.reshape((B, nblk, Q_BLOCK) + t.shape[2:]), 1, 0)

    def block(qs):
        qn, qr = qs
        s = jnp.einsum('bqhe,bkhe->bhqk', qn, k_nope) + jnp.einsum('bqhr,bkr->bhqk', qr, k_rope)
        p = jax.nn.softmax(s.astype(jnp.float32) * scale, axis=-1).astype(v.dtype)
        return jnp.einsum('bhqk,bkhe->bqhe', p, v)

    o = lax.map(block, (to_blocks(q_nope), to_blocks(q_rope)))
    return jnp.moveaxis(o, 0, 1).reshape(B, S, H * MLA_V)


def dilated_window_attention(q, k, v, dilation, n_side):
    B, S, H, E = q.shape
    L = S // dilation
    W = n_side
    nb = -(-L // W)
    Lp = nb * W

    def stride_split(t):
        t = t.reshape(B, L, dilation, H, E).transpose(0, 2, 1, 3, 4)
        return jnp.pad(t, ((0, 0), (0, 0), (0, Lp - L), (0, 0), (0, 0)))

    def band(t):
        tp = jnp.pad(t, ((0, 0), (0, 0), (W, W), (0, 0), (0, 0))).reshape(B, dilation, nb + 2, W, H, E)
        return jnp.concatenate([tp[:, :, :-2], tp[:, :, 1:-1], tp[:, :, 2:]], axis=3)

    qb = stride_split(q).reshape(B, dilation, nb, W, H, E)
    kb = band(stride_split(k))
    vb = band(stride_split(v))
    s = jnp.einsum('bdnqhe,bdnkhe->bdnhqk', qb, kb).astype(jnp.float32) * (E ** -0.5)
    qi = jnp.arange(W)[:, None]
    kj = jnp.arange(3 * W)[None, :]
    rel = kj - W - qi
    t_k = (jnp.arange(nb)[:, None, None] - 1) * W + kj[None]
    valid = (jnp.abs(rel)[None] <= n_side) & (t_k >= 0) & (t_k < L)
    s = jnp.where(valid[:, None], s, NEG_INF)
    lse = jax.nn.logsumexp(s, axis=-1)
    p = jnp.exp(s - lse[..., None]).astype(v.dtype)
    o = jnp.einsum('bdnhqk,bdnkhe->bdnqhe', p, vb)
    o = o.reshape(B, dilation, Lp, H, E)[:, :, :L].transpose(0, 2, 1, 3, 4).reshape(B, S, H, E)
    lse = jnp.moveaxis(lse, 3, -1).reshape(B, dilation, Lp, H)[:, :, :L].transpose(0, 2, 1, 3).reshape(B, S, H)
    return o, lse


def token_mixer(u, cos_m, sin_m, cos_p, sin_p, w_in, g_cq, w_uq, g_ckv, w_ukv, w_o_mla, w_o_dil, w_out):
    B, S, _ = u.shape
    proj = u @ w_in
    offs = np.cumsum(IN_SPLITS)[:-1].tolist()
    c_q, c_kv, k_rope, qkv, gate_a, gate_b = jnp.split(proj, offs, axis=-1)

    q = (rms_norm(c_q, g_cq) @ w_uq).reshape(B, S, MLA_HEADS, MLA_NOPE + MLA_ROPE)
    q_nope = q[..., :MLA_NOPE]
    q_rope = apply_rope(q[..., MLA_NOPE:], cos_m[:, :, None], sin_m[:, :, None])
    kv = (rms_norm(c_kv, g_ckv) @ w_ukv).reshape(B, S, MLA_HEADS, MLA_NOPE + MLA_V)
    k_nope, v_mla = kv[..., :MLA_NOPE], kv[..., MLA_NOPE:]
    k_rope = apply_rope(k_rope, cos_m, sin_m)
    o_mla = mla_attention(q_nope, q_rope, k_nope, k_rope, v_mla)

    qkv = qkv.reshape(B, S, 3, N_DIL_GROUPS, DIL_HEADS_PER_GROUP, DIL_HEAD_DIM)
    cp, sp = cos_p[:, :, None], sin_p[:, :, None]
    outs, lses = [], []
    for g, (window, dil) in enumerate(DIL_PATTERNS):
        qg = partial_rope(qkv[:, :, 0, g], cp, sp)
        kg = partial_rope(qkv[:, :, 1, g], cp, sp)
        o, l = dilated_window_attention(qg, kg, qkv[:, :, 2, g], dil, window // (2 * dil))
        outs.append(o)
        lses.append(l)
    wts = jax.nn.softmax(jnp.stack(lses), axis=0).astype(u.dtype)
    o_dil = jnp.einsum('gbsh,gbshe->bshe', wts, jnp.stack(outs)).reshape(B, S, DIL_HEADS_PER_GROUP * DIL_HEAD_DIM)

    merged = jax.nn.sigmoid(gate_a) * (o_mla @ w_o_mla) + jax.nn.sigmoid(gate_b) * (o_dil @ w_o_dil)
    return merged @ w_out


def ffn_sublayer(x, shift, scale, gate, g_pre, w_gate, w_up, w_down, g_post):
    h = modulate(rms_norm(x, g_pre), shift, scale)
    y = rms_norm(swiglu(h, w_gate, w_up, w_down), g_post)
    return x + FFN_RES * gate[:, None, :] * y


def setup_inputs(seed: int = 0) -> dict:
    key = jax.random.key(seed)
    keys = jax.random.split(key, 25)
    f32 = jnp.float32

    def dense(k, shape, gain=1.0):
        return jax.random.normal(k, (DEPTH,) + shape, f32) * (gain * shape[0] ** -0.5)

    def gain_vec(k, dim):
        return 1.0 + 0.1 * jax.random.normal(k, (DEPTH, dim), f32)

    x = jax.random.normal(keys[0], (BATCH, SEQ, D_MODEL), f32)
    c = jax.random.normal(keys[1], (BATCH, D_MODEL), f32)
    positions = (jax.random.randint(keys[2], (BATCH, 1), 0, 2048, dtype=jnp.int32)
                 + jnp.arange(SEQ, dtype=jnp.int32)[None, :]).astype(jnp.int32)
    n_ada = N_SUBLAYERS * N_MOD * D_MODEL
    return {
        'x': x,
        'c': c,
        'positions': positions,
        'w_ada': dense(keys[3], (D_MODEL, n_ada), 0.5),
        'b_ada': 0.02 * jax.random.normal(keys[4], (DEPTH, n_ada), f32),
        'g_pre_ff1': gain_vec(keys[5], D_MODEL),
        'w_gate1': dense(keys[6], (D_MODEL, D_FF)),
        'w_up1': dense(keys[7], (D_MODEL, D_FF)),
        'w_down1': dense(keys[8], (D_FF, D_MODEL)),
        'g_post_ff1': gain_vec(keys[9], D_MODEL),
        'g_pre_mix': gain_vec(keys[10], D_MODEL),
        'w_in': dense(keys[11], (D_MODEL, D_IN)),
        'g_cq': gain_vec(keys[12], MLA_Q_LORA),
        'w_uq': dense(keys[13], (MLA_Q_LORA, MLA_HEADS * (MLA_NOPE + MLA_ROPE))),
        'g_ckv': gain_vec(keys[14], MLA_KV_LORA),
        'w_ukv': dense(keys[15], (MLA_KV_LORA, MLA_HEADS * (MLA_NOPE + MLA_V))),
        'w_o_mla': dense(keys[16], (MLA_HEADS * MLA_V, D_MODEL)),
        'w_o_dil': dense(keys[17], (DIL_HEADS_PER_GROUP * DIL_HEAD_DIM, D_MODEL)),
        'w_out': dense(keys[18], (D_MODEL, D_MODEL)),
        'g_post_mix': gain_vec(keys[19], D_MODEL),
        'g_pre_ff2': gain_vec(keys[20], D_MODEL),
        'w_gate2': dense(keys[21], (D_MODEL, D_FF)),
        'w_up2': dense(keys[22], (D_MODEL, D_FF)),
        'w_down2': dense(keys[23], (D_FF, D_MODEL)),
        'g_post_ff2': gain_vec(keys[24], D_MODEL),
    }


def reference(x, c, positions, w_ada, b_ada, g_pre_ff1, w_gate1, w_up1, w_down1, g_post_ff1,
              g_pre_mix, w_in, g_cq, w_uq, g_ckv, w_ukv, w_o_mla, w_o_dil, w_out, g_post_mix,
              g_pre_ff2, w_gate2, w_up2, w_down2, g_post_ff2):
    B = x.shape[0]
    cos_m, sin_m = rope_cos_sin(positions, MLA_ROPE, MLA_THETA)
    cos_p, sin_p = rope_cos_sin(positions, ROPE_PART, ROPE_THETA)
    cond = jax.nn.silu(c)
    for l in range(DEPTH):
        ada = (cond @ w_ada[l] + b_ada[l]).reshape(B, N_SUBLAYERS, N_MOD, D_MODEL)
        x = ffn_sublayer(x, ada[:, 0, 0], ada[:, 0, 1], ada[:, 0, 2],
                         g_pre_ff1[l], w_gate1[l], w_up1[l], w_down1[l], g_post_ff1[l])
        u = modulate(rms_norm(x, g_pre_mix[l]), ada[:, 1, 0], ada[:, 1, 1])
        y = token_mixer(u, cos_m, sin_m, cos_p, sin_p, w_in[l], g_cq[l], w_uq[l], g_ckv[l], w_ukv[l],
                        w_o_mla[l], w_o_dil[l], w_out[l])
        x = x + ada[:, 1, 2][:, None, :] * rms_norm(y, g_post_mix[l])
        x = ffn_sublayer(x, ada[:, 2, 0], ada[:, 2, 1], ada[:, 2, 2],
                         g_pre_ff2[l], w_gate2[l], w_up2[l], w_down2[l], g_post_ff2[l])
    return x
```

```python
import functools

import numpy as np
import jax
import jax.numpy as jnp
from jax import lax
from jax.experimental import pallas as pl
from jax.experimental.pallas import tpu as pltpu

F32 = jnp.float32
BF16 = jnp.bfloat16

D_MODEL = 1024
D_FF = 2816
FFN_RES = 0.5
MLA_HEADS = 8
MLA_Q_LORA = 256
MLA_KV_LORA = 128
MLA_NOPE = 64
MLA_ROPE = 32
MLA_V = 64
MLA_THETA = 10000.0
DIL_PATTERNS = ((128, 1), (512, 4), (2048, 16))
N_DIL_GROUPS = 3
DIL_HEADS = 4
DIL_HEAD_DIM = 64
DIL_GROUP_W = DIL_HEADS * DIL_HEAD_DIM
DIL_QKV = 3 * N_DIL_GROUPS * DIL_GROUP_W
ROPE_THETA = 500000.0
ROPE_PART = DIL_HEAD_DIM // 4
NORM_EPS = 1e-6
NEG_INF = -1e30

LANES = 128
MXU_N = 256
FF_CHUNK = MXU_N
N_FF_CHUNKS = D_FF // FF_CHUNK
MLA_PAD = LANES
VMEM_LIMIT = 56 * 1024 * 1024

ROW_INVF_Q, ROW_MC_Q, ROW_MS_Q, ROW_INVF_K, ROW_MC_K, ROW_MS_K, ROW_INVF_D, ROW_M1_D, ROW_M2_D = range(9)
N_TABLE_ROWS = 16


def _rms(x, g):
    return x * lax.rsqrt(jnp.mean(x * x, axis=-1, keepdims=True) + NORM_EPS) * g


def _silu(x):
    return x * jax.nn.sigmoid(x)


def _dot(a, b):
    return jnp.dot(a, b, preferred_element_type=F32)


def _dot_nt(a, b):
    return lax.dot_general(a, b, (((1,), (1,)), ((), ())), preferred_element_type=F32)


def _const_spec(shape):
    nd = len(shape)
    return pl.BlockSpec(shape, lambda *_: (0,) * nd, pipeline_mode=pl.Buffered(1))


def _ada_kernel(c_ref, w_ref, b_ref, o_ref):
    cond = _silu(c_ref[...])
    o_ref[...] = _dot(cond, w_ref[...]) + b_ref[...]


def _ada(c, w, b):
    bsz, d = c.shape
    n = w.shape[1]
    tn = 1024
    return pl.pallas_call(
        _ada_kernel,
        grid=(n // tn,),
        in_specs=[
            pl.BlockSpec((bsz, d), lambda j: (0, 0)),
            pl.BlockSpec((d, tn), lambda j: (0, j)),
            pl.BlockSpec((1, tn), lambda j: (0, j)),
        ],
        out_specs=pl.BlockSpec((bsz, tn), lambda j: (0, j)),
        out_shape=jax.ShapeDtypeStruct((bsz, n), F32),
        compiler_params=pltpu.CompilerParams(dimension_semantics=("arbitrary",)),
        name="ada",
    )(c, w, b.reshape(1, n))


def _ffn_kernel(x_ref, shift_ref, scale_ref, gate_ref, gpre_ref, gpost_ref,
                wg_ref, wu_ref, wd_ref, o_ref):
    x = x_ref[...]
    h = _rms(x, gpre_ref[...]) * (1.0 + scale_ref[...]) + shift_ref[...]
    hb = h.astype(BF16)
    acc = None
    for c in range(N_FF_CHUNKS):
        g = _dot(hb, wg_ref[c])
        u = _dot(hb, wu_ref[c])
        a = (_silu(g) * u).astype(BF16)
        d = _dot(a, wd_ref[c])
        acc = d if acc is None else acc + d
    y = _rms(acc, gpost_ref[...])
    o_ref[...] = x + (FFN_RES * gate_ref[...]) * y


def _ffn(x, shift, scale, gate, g_pre, g_post, wg, wu, wd, tm=512):
    bsz, s, d = x.shape
    tok = pl.BlockSpec((None, tm, d), lambda b, i: (b, i, 0))
    mod = pl.BlockSpec((None, 1, d), lambda b, i: (b, 0, 0))
    return pl.pallas_call(
        _ffn_kernel,
        grid=(bsz, s // tm),
        in_specs=[tok, mod, mod, mod, _const_spec((1, d)), _const_spec((1, d)),
                  _const_spec(wg.shape), _const_spec(wu.shape), _const_spec(wd.shape)],
        out_specs=tok,
        out_shape=jax.ShapeDtypeStruct(x.shape, F32),
        compiler_params=pltpu.CompilerParams(
            dimension_semantics=("arbitrary", "arbitrary"), vmem_limit_bytes=VMEM_LIMIT),
        name="ffn",
    )(x, shift, scale, gate, g_pre, g_post, wg, wu, wd)


def _proj_kernel(x_ref, pos_ref, shift_ref, scale_ref, gpre_ref, tab_ref, win_ref,
                 gcq_ref, wuq_ref, gckv_ref, wkv_ref,
                 q_ref, k_ref, v_ref, d_ref):
    x = x_ref[...]
    u = (_rms(x, gpre_ref[...]) * (1.0 + scale_ref[...]) + shift_ref[...]).astype(BF16)
    pos = pos_ref[...]

    def row(r):
        return tab_ref[r:r + 1, :]

    c_q = _dot(u, win_ref[:, 0:MLA_Q_LORA])
    q_pad = _dot(_rms(c_q, gcq_ref[...]).astype(BF16), wuq_ref[...])
    ang_q = pos * row(ROW_INVF_Q)
    t_q = jnp.cos(ang_q) * row(ROW_MC_Q) + jnp.sin(ang_q) * row(ROW_MS_Q)
    for h in range(MLA_HEADS):
        sl = slice(h * MLA_PAD, (h + 1) * MLA_PAD)
        q_ref[:, sl] = (q_pad[:, sl] * t_q).astype(BF16)

    pb = _dot(u, win_ref[:, MLA_Q_LORA:2 * MLA_Q_LORA])
    c_kv = pb[:, :MLA_KV_LORA]
    r_kv = lax.rsqrt(jnp.mean(c_kv * c_kv, axis=-1, keepdims=True) + NORM_EPS)
    ang_k = pos * row(ROW_INVF_K)
    t_k = jnp.cos(ang_k) * row(ROW_MC_K) + jnp.sin(ang_k) * row(ROW_MS_K)
    lhs = jnp.concatenate([c_kv * r_kv * gckv_ref[...], pb[:, MLA_KV_LORA:] * t_k], axis=-1)
    kv = _dot(lhs.astype(BF16), wkv_ref[...])
    k_ref[...] = kv[:, :MLA_HEADS * MLA_PAD].astype(BF16)
    v_ref[...] = kv[:, MLA_HEADS * MLA_PAD:].astype(BF16)

    ang_d = pos * row(ROW_INVF_D)
    cd = jnp.cos(ang_d)
    sd = jnp.sin(ang_d)
    s1 = sd * row(ROW_M1_D)
    s2 = sd * row(ROW_M2_D)
    q_scale = DIL_HEAD_DIM ** -0.5
    base = 2 * MLA_Q_LORA
    n_chunks = DIL_QKV // MXU_N
    rope_chunks = 2 * N_DIL_GROUPS * DIL_GROUP_W // MXU_N
    q_chunks = N_DIL_GROUPS * DIL_GROUP_W // MXU_N
    for j in range(n_chunks):
        pc = _dot(u, win_ref[:, base + j * MXU_N: base + (j + 1) * MXU_N])
        if j < rope_chunks:
            for half in range(MXU_N // LANES):
                xs = pc[:, half * LANES:(half + 1) * LANES]
                rot = (xs * cd + pltpu.roll(xs, ROPE_PART // 2, 1) * s1
                       + pltpu.roll(xs, LANES - ROPE_PART // 2, 1) * s2)
                if j < q_chunks:
                    rot = rot * q_scale
                lo = j * MXU_N + half * LANES
                d_ref[:, lo:lo + LANES] = rot.astype(BF16)
        else:
            d_ref[:, j * MXU_N:(j + 1) * MXU_N] = pc.astype(BF16)


def _proj(x, pos, shift, scale, g_pre, tables, w_in2, g_cq, w_uq2, g_ckv, w_kv, tm=512):
    bsz, s, d = x.shape
    tok = lambda n: pl.BlockSpec((None, tm, n), lambda b, i: (b, i, 0))
    mod = pl.BlockSpec((None, 1, d), lambda b, i: (b, 0, 0))
    out_shapes = (
        jax.ShapeDtypeStruct((bsz, s, MLA_HEADS * MLA_PAD), BF16),
        jax.ShapeDtypeStruct((bsz, s, MLA_HEADS * MLA_PAD), BF16),
        jax.ShapeDtypeStruct((bsz, s, MLA_HEADS * MLA_V), BF16),
        jax.ShapeDtypeStruct((bsz, s, DIL_QKV), BF16),
    )
    return pl.pallas_call(
        _proj_kernel,
        grid=(bsz, s // tm),
        in_specs=[tok(d), tok(1), mod, mod, _const_spec((1, d)), _const_spec(tables.shape),
                  _const_spec(w_in2.shape), _const_spec(g_cq.shape), _const_spec(w_uq2.shape),
                  _const_spec(g_ckv.shape), _const_spec(w_kv.shape)],
        out_specs=(tok(MLA_HEADS * MLA_PAD), tok(MLA_HEADS * MLA_PAD), tok(MLA_HEADS * MLA_V),
                   tok(DIL_QKV)),
        out_shape=out_shapes,
        compiler_params=pltpu.CompilerParams(
            dimension_semantics=("arbitrary", "arbitrary"), vmem_limit_bytes=VMEM_LIMIT),
        name="proj",
    )(x, pos, shift, scale, g_pre, tables, w_in2, g_cq, w_uq2, g_ckv, w_kv)


def _mla_kernel(q_ref, k_ref, v_ref, o_ref, *, tk):
    tq = q_ref.shape[0]
    n_kv = k_ref.shape[0] // tk
    lane = lax.broadcasted_iota(jnp.int32, (1, LANES), 1)
    for pair in range(MLA_HEADS // 2):
        outs = []
        for h in (2 * pair, 2 * pair + 1):
            hs = slice(h * MLA_PAD, (h + 1) * MLA_PAD)
            vs = slice(pair * LANES, (pair + 1) * LANES)
            q = q_ref[:, hs]

            def body(j, carry, hs=hs, vs=vs, q=q):
                m, l, acc = carry
                rows = pl.ds(pl.multiple_of(j * tk, tk), tk)
                s = _dot_nt(q, k_ref[rows, hs])
                m_new = jnp.maximum(m, jnp.max(s, axis=-1, keepdims=True))
                alpha = jnp.exp(m - m_new)
                p = jnp.exp(s - m_new)
                l = alpha * l + jnp.sum(p, axis=-1, keepdims=True)
                acc = alpha * acc + _dot(p.astype(BF16), v_ref[rows, vs])
                return m_new, l, acc

            init = (jnp.full((tq, 1), NEG_INF, F32), jnp.zeros((tq, 1), F32),
                    jnp.zeros((tq, LANES), F32))
            m, l, acc = lax.fori_loop(0, n_kv, body, init)
            outs.append(acc / l)
        merged = jnp.where(lane < MLA_V, outs[0], outs[1])
        o_ref[:, pair * LANES:(pair + 1) * LANES] = merged.astype(BF16)


def _mla(q, k, v, tq=512, tk=512):
    bsz, s, _ = q.shape
    return pl.pallas_call(
        functools.partial(_mla_kernel, tk=tk),
        grid=(bsz, s // tq),
        in_specs=[
            pl.BlockSpec((None, tq, q.shape[2]), lambda b, i: (b, i, 0)),
            pl.BlockSpec((None, s, k.shape[2]), lambda b, i: (b, 0, 0)),
            pl.BlockSpec((None, s, v.shape[2]), lambda b, i: (b, 0, 0)),
        ],
        out_specs=pl.BlockSpec((None, tq, v.shape[2]), lambda b, i: (b, i, 0)),
        out_shape=jax.ShapeDtypeStruct(v.shape, BF16),
        compiler_params=pltpu.CompilerParams(
            dimension_semantics=("arbitrary", "arbitrary"), vmem_limit_bytes=VMEM_LIMIT),
        name="mla",
    )(q, k, v)


def _dil_kernel(q_ref, k_ref, v_ref, o_ref, lse_ref, *, n_side, win):
    tq = q_ref.shape[0]
    seq = k_ref.shape[0]
    qs = pl.program_id(1) * tq
    if win < seq:
        ks = pl.multiple_of(jnp.clip(qs - n_side, 0, seq - win), n_side)
    else:
        ks = 0
    q = q_ref[...]
    kw = k_ref[pl.ds(ks, win), :]
    vw = v_ref[pl.ds(ks, win), :]
    qpos = qs + lax.broadcasted_iota(jnp.int32, (tq, 1), 0)
    kpos = ks + lax.broadcasted_iota(jnp.int32, (1, win), 1)
    valid = jnp.abs(kpos - qpos) <= n_side
    lane_head = lax.broadcasted_iota(jnp.int32, (1, DIL_GROUP_W), 1) // DIL_HEAD_DIM
    o = jnp.zeros((tq, DIL_GROUP_W), F32)
    lse = jnp.zeros((tq, DIL_GROUP_W), F32)
    for h in range(DIL_HEADS):
        hm = lane_head == h
        qh = jnp.where(hm, q, jnp.zeros_like(q))
        s = jnp.where(valid, _dot_nt(qh, kw), NEG_INF)
        m = jnp.max(s, axis=-1, keepdims=True)
        p = jnp.exp(s - m)
        l = jnp.sum(p, axis=-1, keepdims=True)
        pv = _dot(p.astype(BF16), vw)
        o = jnp.where(hm, pv / l, o)
        lse = jnp.where(hm, m + jnp.log(l), lse)
    o_ref[...] = o.astype(BF16)
    lse_ref[...] = lse


def _dilated(q, k, v, n_side, tq=256):
    nseq, seq, w = q.shape
    win = min(seq, tq + 2 * n_side)
    qspec = pl.BlockSpec((None, tq, w), lambda b, i: (b, i, 0))
    kvspec = pl.BlockSpec((None, seq, w), lambda b, i: (b, 0, 0))
    return pl.pallas_call(
        functools.partial(_dil_kernel, n_side=n_side, win=win),
        grid=(nseq, seq // tq),
        in_specs=[qspec, kvspec, kvspec],
        out_specs=(qspec, qspec),
        out_shape=(jax.ShapeDtypeStruct(q.shape, BF16), jax.ShapeDtypeStruct(q.shape, F32)),
        compiler_params=pltpu.CompilerParams(
            dimension_semantics=("arbitrary", "arbitrary"), vmem_limit_bytes=VMEM_LIMIT),
        name="dilated",
    )(q, k, v)


def _post_kernel(x_ref, shift_ref, scale_ref, gate_ref, gpre_ref, gpost_ref, wgate_ref,
                 omla_ref, o0_ref, o1_ref, o2_ref, l0_ref, l1_ref, l2_ref,
                 womla_ref, wodil_ref, wout_ref, out_ref):
    x = x_ref[...]
    u = (_rms(x, gpre_ref[...]) * (1.0 + scale_ref[...]) + shift_ref[...]).astype(BF16)
    gate_a = jax.nn.sigmoid(_dot(u, wgate_ref[:, :D_MODEL]))
    gate_b = jax.nn.sigmoid(_dot(u, wgate_ref[:, D_MODEL:]))
    l0, l1, l2 = l0_ref[...], l1_ref[...], l2_ref[...]
    mx = jnp.maximum(jnp.maximum(l0, l1), l2)
    e0, e1, e2 = jnp.exp(l0 - mx), jnp.exp(l1 - mx), jnp.exp(l2 - mx)
    inv = 1.0 / (e0 + e1 + e2)
    o_dil = ((e0 * inv) * o0_ref[...].astype(F32) + (e1 * inv) * o1_ref[...].astype(F32)
             + (e2 * inv) * o2_ref[...].astype(F32))
    merged = (gate_a * _dot(omla_ref[...], womla_ref[...])
              + gate_b * _dot(o_dil.astype(BF16), wodil_ref[...]))
    y = _dot(merged.astype(BF16), wout_ref[...])
    out_ref[...] = x + gate_ref[...] * _rms(y, gpost_ref[...])


def _post(x, shift, scale, gate, g_pre, g_post, w_gates, o_mla, o_dil, lse_dil,
          w_o_mla, w_o_dil, w_out, tm=512):
    bsz, s, d = x.shape
    tok = lambda n: pl.BlockSpec((None, tm, n), lambda b, i: (b, i, 0))
    mod = pl.BlockSpec((None, 1, d), lambda b, i: (b, 0, 0))
    gw = DIL_GROUP_W
    return pl.pallas_call(
        _post_kernel,
        grid=(bsz, s // tm),
        in_specs=[tok(d), mod, mod, mod, _const_spec((1, d)), _const_spec((1, d)),
                  _const_spec(w_gates.shape), tok(o_mla.shape[2]),
                  tok(gw), tok(gw), tok(gw), tok(gw), tok(gw), tok(gw),
                  _const_spec(w_o_mla.shape), _const_spec(w_o_dil.shape),
                  _const_spec(w_out.shape)],
        out_specs=tok(d),
        out_shape=jax.ShapeDtypeStruct(x.shape, F32),
        compiler_params=pltpu.CompilerParams(
            dimension_semantics=("arbitrary", "arbitrary"), vmem_limit_bytes=VMEM_LIMIT),
        name="post",
    )(x, shift, scale, gate, g_pre, g_post, w_gates, o_mla, *o_dil, *lse_dil,
      w_o_mla, w_o_dil, w_out)


def _tables():
    t = np.zeros((N_TABLE_ROWS, LANES), np.float32)
    scale = (MLA_NOPE + MLA_ROPE) ** -0.5
    half = MLA_ROPE // 2
    f_m = MLA_THETA ** (-np.arange(0, MLA_ROPE, 2, dtype=np.float32) / MLA_ROPE)
    t[ROW_INVF_Q, MLA_NOPE:] = np.tile(f_m, 4)
    t[ROW_MC_Q, :MLA_NOPE + MLA_ROPE] = scale
    t[ROW_MS_Q, MLA_NOPE + MLA_ROPE:MLA_NOPE + MLA_ROPE + half] = -scale
    t[ROW_MS_Q, MLA_NOPE + MLA_ROPE + half:] = scale
    t[ROW_INVF_K, :2 * MLA_ROPE] = np.tile(f_m, 4)
    t[ROW_MC_K, :MLA_ROPE] = 1.0
    t[ROW_MS_K, MLA_ROPE:MLA_ROPE + half] = -1.0
    t[ROW_MS_K, MLA_ROPE + half:2 * MLA_ROPE] = 1.0
    f_d = ROPE_THETA ** (-np.arange(0, ROPE_PART, 2, dtype=np.float32) / ROPE_PART)
    hp = ROPE_PART // 2
    for off in (0, DIL_HEAD_DIM):
        t[ROW_INVF_D, off:off + ROPE_PART] = np.tile(f_d, 2)
        t[ROW_M2_D, off:off + hp] = -1.0
        t[ROW_M1_D, off + hp:off + ROPE_PART] = 1.0
    return jnp.asarray(t)


def _layout_weights(w_in, w_uq, w_ukv):
    o_ckv = MLA_Q_LORA
    o_kr = o_ckv + MLA_KV_LORA
    o_qkv = o_kr + MLA_ROPE
    o_ga = o_qkv + DIL_QKV
    half = MLA_ROPE // 2
    k_raw = w_in[:, o_kr:o_qkv]
    k_swap = jnp.concatenate([k_raw[:, half:], k_raw[:, :half]], axis=1)
    pad = jnp.zeros((D_MODEL, 2 * MLA_Q_LORA - MLA_Q_LORA - MLA_KV_LORA - 2 * MLA_ROPE), w_in.dtype)
    w_in2 = jnp.concatenate(
        [w_in[:, :o_kr], k_raw, k_swap, pad, w_in[:, o_qkv:o_ga]], axis=1).astype(BF16)
    w_gates = w_in[:, o_ga:].astype(BF16)

    wq = w_uq.reshape(MLA_Q_LORA, MLA_HEADS, MLA_NOPE + MLA_ROPE)
    nope, t1, t2 = wq[..., :MLA_NOPE], wq[..., MLA_NOPE:MLA_NOPE + half], wq[..., MLA_NOPE + half:]
    w_uq2 = jnp.concatenate([nope, t1, t2, t2, t1], axis=-1).reshape(
        MLA_Q_LORA, MLA_HEADS * MLA_PAD).astype(BF16)

    wkv = w_ukv.reshape(MLA_KV_LORA, MLA_HEADS, MLA_NOPE + MLA_V)
    k_cols = jnp.concatenate(
        [wkv[..., :MLA_NOPE], jnp.zeros((MLA_KV_LORA, MLA_HEADS, MLA_PAD - MLA_NOPE), w_ukv.dtype)],
        axis=-1).reshape(MLA_KV_LORA, MLA_HEADS * MLA_PAD)
    v_cols = wkv[..., MLA_NOPE:].reshape(MLA_KV_LORA, MLA_HEADS * MLA_V)
    place = np.zeros((MLA_ROPE, MLA_HEADS, MLA_PAD), np.float32)
    for i in range(MLA_ROPE):
        place[i, :, MLA_NOPE + i] = 1.0
        place[i, :, MLA_NOPE + MLA_ROPE + i] = 1.0
    place = jnp.asarray(place.reshape(MLA_ROPE, MLA_HEADS * MLA_PAD))
    n_lhs = 2 * MLA_Q_LORA - MLA_Q_LORA
    k_rows = jnp.concatenate(
        [k_cols, place, place,
         jnp.zeros((n_lhs - MLA_KV_LORA - 2 * MLA_ROPE, MLA_HEADS * MLA_PAD), F32)], axis=0)
    v_rows = jnp.concatenate(
        [v_cols, jnp.zeros((n_lhs - MLA_KV_LORA, MLA_HEADS * MLA_V), F32)], axis=0)
    w_kv = jnp.concatenate([k_rows, v_rows], axis=1).astype(BF16)
    return w_in2, w_gates, w_uq2, w_kv


def _ff_weights(w_gate, w_up, w_down):
    wg = w_gate.astype(BF16).reshape(D_MODEL, N_FF_CHUNKS, FF_CHUNK).transpose(1, 0, 2)
    wu = w_up.astype(BF16).reshape(D_MODEL, N_FF_CHUNKS, FF_CHUNK).transpose(1, 0, 2)
    wd = w_down.astype(BF16).reshape(N_FF_CHUNKS, FF_CHUNK, D_MODEL)
    return wg, wu, wd


def _stride_split(t, dil):
    bsz, s, w = t.shape
    return t.reshape(bsz, s // dil, dil, w).transpose(0, 2, 1, 3).reshape(bsz * dil, s // dil, w)


def _stride_merge(t, dil, bsz):
    _, seq, w = t.shape
    return t.reshape(bsz, dil, seq, w).transpose(0, 2, 1, 3).reshape(bsz, seq * dil, w)


def kernel(x, c, positions, w_ada, b_ada, g_pre_ff1, w_gate1, w_up1, w_down1, g_post_ff1, g_pre_mix, w_in, g_cq, w_uq, g_ckv, w_ukv, w_o_mla, w_o_dil, w_out, g_post_mix, g_pre_ff2, w_gate2, w_up2, w_down2, g_post_ff2):
    bsz, s, d = x.shape
    depth = w_ada.shape[0]
    pos = positions.astype(F32).reshape(bsz, s, 1)
    tables = _tables()
    for l in range(depth):
        ada = _ada(c, w_ada[l], b_ada[l]).reshape(bsz, 3, 3, 1, d)
        mod = lambda i, j: ada[:, i, j]

        x = _ffn(x, mod(0, 0), mod(0, 1), mod(0, 2), g_pre_ff1[l][None], g_post_ff1[l][None],
                 *_ff_weights(w_gate1[l], w_up1[l], w_down1[l]))

        w_in2, w_gates, w_uq2, w_kv = _layout_weights(w_in[l], w_uq[l], w_ukv[l])
        q_m, k_m, v_m, dqkv = _proj(x, pos, mod(1, 0), mod(1, 1), g_pre_mix[l][None], tables,
                                    w_in2, g_cq[l][None], w_uq2, g_ckv[l][None], w_kv)
        o_mla = _mla(q_m, k_m, v_m)

        outs, lses = [], []
        gq = N_DIL_GROUPS * DIL_GROUP_W
        for g, (window, dil) in enumerate(DIL_PATTERNS):
            parts = [_stride_split(dqkv[:, :, w * gq + g * DIL_GROUP_W: w * gq + (g + 1) * DIL_GROUP_W], dil)
                     for w in range(3)]
            o_g, lse_g = _dilated(*parts, n_side=window // (2 * dil))
            outs.append(_stride_merge(o_g, dil, bsz))
            lses.append(_stride_merge(lse_g, dil, bsz))

        x = _post(x, mod(1, 0), mod(1, 1), mod(1, 2), g_pre_mix[l][None], g_post_mix[l][None],
                  w_gates, o_mla, outs, lses,
                  w_o_mla[l].astype(BF16), w_o_dil[l].astype(BF16), w_out[l].astype(BF16))

        x = _ffn(x, mod(2, 0), mod(2, 1), mod(2, 2), g_pre_ff2[l][None], g_post_ff2[l][None],
                 *_ff_weights(w_gate2[l], w_up2[l], w_down2[l]))
    return x
```

```python
import functools

import numpy as np
import jax
import jax.numpy as jnp
from jax import lax
from jax.experimental import pallas as pl
from jax.experimental.pallas import tpu as pltpu

F32 = jnp.float32
BF16 = jnp.bfloat16

D_MODEL = 1024
D_FF = 2816
FFN_RES = 0.5
MLA_HEADS = 8
MLA_Q_LORA = 256
MLA_KV_LORA = 128
MLA_NOPE = 64
MLA_ROPE = 32
MLA_V = 64
MLA_THETA = 10000.0
DIL_PATTERNS = ((128, 1), (512, 4), (2048, 16))
N_DIL_GROUPS = 3
DIL_HEADS = 4
DIL_HEAD_DIM = 64
DIL_GROUP_W = DIL_HEADS * DIL_HEAD_DIM
DIL_QKV = 3 * N_DIL_GROUPS * DIL_GROUP_W
ROPE_THETA = 500000.0
ROPE_PART = DIL_HEAD_DIM // 4
NORM_EPS = 1e-6
NEG_INF = -1e30

LANES = 128
MXU_N = 256
FF_CHUNK = MXU_N
N_FF_CHUNKS = D_FF // FF_CHUNK
MLA_PAD = LANES
MLA_PAIRS = MLA_HEADS * MLA_V // LANES
LOG2E = float(np.log2(np.e))
VMEM_LIMIT = 56 * 1024 * 1024

ROW_INVF_Q, ROW_MC_Q, ROW_MS_Q, ROW_INVF_K, ROW_MC_K, ROW_MS_K, ROW_INVF_D, ROW_M1_D, ROW_M2_D = range(9)
N_TABLE_ROWS = 16


def _rms(x, g):
    return x * lax.rsqrt(jnp.mean(x * x, axis=-1, keepdims=True) + NORM_EPS) * g


def _silu(x):
    return x * jax.nn.sigmoid(x)


def _dot(a, b):
    return jnp.dot(a, b, preferred_element_type=F32)


def _dot_nt(a, b):
    return lax.dot_general(a, b, (((1,), (1,)), ((), ())), preferred_element_type=F32)


def _const_spec(shape):
    nd = len(shape)
    return pl.BlockSpec(shape, lambda *_: (0,) * nd, pipeline_mode=pl.Buffered(1))


def _ada_kernel(c_ref, w_ref, b_ref, o_ref):
    cond = _silu(c_ref[...])
    o_ref[...] = _dot(cond, w_ref[...]) + b_ref[...]


def _ada(c, w, b):
    bsz, d = c.shape
    n = w.shape[1]
    tn = 1024
    return pl.pallas_call(
        _ada_kernel,
        grid=(n // tn,),
        in_specs=[
            pl.BlockSpec((bsz, d), lambda j: (0, 0)),
            pl.BlockSpec((d, tn), lambda j: (0, j)),
            pl.BlockSpec((1, tn), lambda j: (0, j)),
        ],
        out_specs=pl.BlockSpec((bsz, tn), lambda j: (0, j)),
        out_shape=jax.ShapeDtypeStruct((bsz, n), F32),
        compiler_params=pltpu.CompilerParams(dimension_semantics=("arbitrary",)),
        name="ada",
    )(c, w, b.reshape(1, n))


def _ffn_kernel(x_ref, shift_ref, scale_ref, gate_ref, gpre_ref, gpost_ref,
                wg_ref, wu_ref, wd_ref, o_ref):
    x = x_ref[...]
    h = _rms(x, gpre_ref[...]) * (1.0 + scale_ref[...]) + shift_ref[...]
    hb = h.astype(BF16)
    acc = None
    for c in range(N_FF_CHUNKS):
        g = _dot(hb, wg_ref[c])
        u = _dot(hb, wu_ref[c])
        a = (_silu(g) * u).astype(BF16)
        d = _dot(a, wd_ref[c])
        acc = d if acc is None else acc + d
    y = _rms(acc, gpost_ref[...])
    o_ref[...] = x + (FFN_RES * gate_ref[...]) * y


def _ffn(x, shift, scale, gate, g_pre, g_post, wg, wu, wd, tm=512):
    bsz, s, d = x.shape
    tok = pl.BlockSpec((None, tm, d), lambda b, i: (b, i, 0))
    mod = pl.BlockSpec((None, 1, d), lambda b, i: (b, 0, 0))
    return pl.pallas_call(
        _ffn_kernel,
        grid=(bsz, s // tm),
        in_specs=[tok, mod, mod, mod, _const_spec((1, d)), _const_spec((1, d)),
                  _const_spec(wg.shape), _const_spec(wu.shape), _const_spec(wd.shape)],
        out_specs=tok,
        out_shape=jax.ShapeDtypeStruct(x.shape, F32),
        compiler_params=pltpu.CompilerParams(
            dimension_semantics=("arbitrary", "arbitrary"), vmem_limit_bytes=VMEM_LIMIT),
        name="ffn",
    )(x, shift, scale, gate, g_pre, g_post, wg, wu, wd)


def _proj_kernel(x_ref, pos_ref, shift_ref, scale_ref, gpre_ref, tab_ref, win_ref,
                 gcq_ref, wuq_ref, gckv_ref, wkv_ref,
                 q_ref, k_ref, v_ref, d_ref):
    x = x_ref[...]
    u = (_rms(x, gpre_ref[...]) * (1.0 + scale_ref[...]) + shift_ref[...]).astype(BF16)
    pos = pos_ref[...]

    def row(r):
        return tab_ref[r:r + 1, :]

    c_q = _dot(u, win_ref[:, 0:MLA_Q_LORA])
    q_pad = _dot(_rms(c_q, gcq_ref[...]).astype(BF16), wuq_ref[...])
    ang_q = pos * row(ROW_INVF_Q)
    t_q = jnp.cos(ang_q) * row(ROW_MC_Q) + jnp.sin(ang_q) * row(ROW_MS_Q)
    for h in range(MLA_HEADS):
        q_ref[h] = (q_pad[:, h * MLA_PAD:(h + 1) * MLA_PAD] * t_q).astype(BF16)

    pb = _dot(u, win_ref[:, MLA_Q_LORA:2 * MLA_Q_LORA])
    c_kv = pb[:, :MLA_KV_LORA]
    r_kv = lax.rsqrt(jnp.mean(c_kv * c_kv, axis=-1, keepdims=True) + NORM_EPS)
    ang_k = pos * row(ROW_INVF_K)
    t_k = jnp.cos(ang_k) * row(ROW_MC_K) + jnp.sin(ang_k) * row(ROW_MS_K)
    lhs = jnp.concatenate([c_kv * r_kv * gckv_ref[...], pb[:, MLA_KV_LORA:] * t_k], axis=-1)
    kv = _dot(lhs.astype(BF16), wkv_ref[...])
    for h in range(MLA_HEADS):
        k_ref[h] = kv[:, h * MLA_PAD:(h + 1) * MLA_PAD].astype(BF16)
    v_off = MLA_HEADS * MLA_PAD
    for p in range(MLA_PAIRS):
        v_ref[p] = kv[:, v_off + p * LANES: v_off + (p + 1) * LANES].astype(BF16)

    ang_d = pos * row(ROW_INVF_D)
    cd = jnp.cos(ang_d)
    sd = jnp.sin(ang_d)
    s1 = sd * row(ROW_M1_D)
    s2 = sd * row(ROW_M2_D)
    q_scale = DIL_HEAD_DIM ** -0.5
    base = 2 * MLA_Q_LORA
    n_chunks = DIL_QKV // MXU_N
    rope_chunks = 2 * N_DIL_GROUPS * DIL_GROUP_W // MXU_N
    q_chunks = N_DIL_GROUPS * DIL_GROUP_W // MXU_N
    for j in range(n_chunks):
        pc = _dot(u, win_ref[:, base + j * MXU_N: base + (j + 1) * MXU_N])
        if j < rope_chunks:
            for half in range(MXU_N // LANES):
                xs = pc[:, half * LANES:(half + 1) * LANES]
                rot = (xs * cd + pltpu.roll(xs, ROPE_PART // 2, 1) * s1
                       + pltpu.roll(xs, LANES - ROPE_PART // 2, 1) * s2)
                if j < q_chunks:
                    rot = rot * q_scale
                lo = j * MXU_N + half * LANES
                d_ref[:, lo:lo + LANES] = rot.astype(BF16)
        else:
            d_ref[:, j * MXU_N:(j + 1) * MXU_N] = pc.astype(BF16)


def _proj(x, pos, shift, scale, g_pre, tables, w_in2, g_cq, w_uq2, g_ckv, w_kv, tm=512):
    bsz, s, d = x.shape
    tok = lambda n: pl.BlockSpec((None, tm, n), lambda b, i: (b, i, 0))
    mod = pl.BlockSpec((None, 1, d), lambda b, i: (b, 0, 0))
    heads = lambda n: pl.BlockSpec((None, n, tm, LANES), lambda b, i: (b, 0, i, 0))
    out_shapes = (
        jax.ShapeDtypeStruct((bsz, MLA_HEADS, s, MLA_PAD), BF16),
        jax.ShapeDtypeStruct((bsz, MLA_HEADS, s, MLA_PAD), BF16),
        jax.ShapeDtypeStruct((bsz, MLA_PAIRS, s, LANES), BF16),
        jax.ShapeDtypeStruct((bsz, s, DIL_QKV), BF16),
    )
    return pl.pallas_call(
        _proj_kernel,
        grid=(bsz, s // tm),
        in_specs=[tok(d), tok(1), mod, mod, _const_spec((1, d)), _const_spec(tables.shape),
                  _const_spec(w_in2.shape), _const_spec(g_cq.shape), _const_spec(w_uq2.shape),
                  _const_spec(g_ckv.shape), _const_spec(w_kv.shape)],
        out_specs=(heads(MLA_HEADS), heads(MLA_HEADS), heads(MLA_PAIRS), tok(DIL_QKV)),
        out_shape=out_shapes,
        compiler_params=pltpu.CompilerParams(
            dimension_semantics=("arbitrary", "arbitrary"), vmem_limit_bytes=VMEM_LIMIT),
        name="proj",
    )(x, pos, shift, scale, g_pre, tables, w_in2, g_cq, w_uq2, g_ckv, w_kv)


def _mla_kernel(q_ref, k_ref, v_ref, o_ref, s_even, s_odd, p_even, p_odd, acc_scr, *, tk):
    tq = q_ref.shape[1]
    n_kv = k_ref.shape[1] // tk
    kv_bits = n_kv.bit_length() - 1
    assert n_kv == 1 << kv_bits and n_kv % 2 == 0
    n_steps = MLA_HEADS * n_kv
    lane = lax.broadcasted_iota(jnp.int32, (1, LANES), 1)
    ones = jnp.ones((tk, LANES), BF16)

    def split(t):
        return lax.shift_right_logical(t, kv_bits), lax.bitwise_and(t, n_kv - 1)

    def scores(t, s_scr):
        h, j = split(t)
        rows = pl.ds(pl.multiple_of(j * tk, tk), tk)
        s_scr[...] = _dot_nt(q_ref[h], k_ref[h, rows, :])

    def softmax(t, m, s_scr, p_scr):
        _, j = split(t)
        s = s_scr[...]
        m_prev = jnp.where(j == 0, NEG_INF, m)
        m_new = jnp.maximum(m_prev, jnp.max(s, axis=-1, keepdims=True))
        p_scr[...] = jnp.exp2(s - m_new).astype(BF16)
        return m_new, jnp.exp2(m_prev - m_new)

    def values(t, alpha, p_scr, may_finish):
        h, j = split(t)
        pair = lax.shift_right_logical(h, 1)
        rows = pl.ds(pl.multiple_of(j * tk, tk), tk)
        rhs = jnp.concatenate([v_ref[pair, rows, :], ones], axis=1)
        acc = alpha * acc_scr[...] + _dot(p_scr[...], rhs)
        acc_scr[...] = acc
        if may_finish:
            @pl.when(j == n_kv - 1)
            def _():
                o = acc[:, :LANES] / acc[:, LANES:]
                odd = lax.bitwise_and(h, 1) == 1
                keep = jnp.logical_and(odd, lane < MLA_V)
                o_ref[pair] = jnp.where(keep, o_ref[pair].astype(F32), o).astype(BF16)

    def body(u, carry):
        m, alpha_e = carry
        t = 2 * u + 1
        scores(t + 1, s_even)
        m, alpha_o = softmax(t, m, s_odd, p_odd)
        values(t - 1, alpha_e, p_even, False)
        scores(t + 2, s_odd)
        m, alpha_e = softmax(t + 1, m, s_even, p_even)
        values(t, alpha_o, p_odd, True)
        return m, alpha_e

    o_ref[...] = jnp.zeros(o_ref.shape, BF16)
    acc_scr[...] = jnp.zeros(acc_scr.shape, F32)
    scores(jnp.int32(0), s_even)
    scores(jnp.int32(1), s_odd)
    carry = softmax(jnp.int32(0), jnp.full((tq, 1), NEG_INF, F32), s_even, p_even)
    m, alpha_e = lax.fori_loop(0, n_steps // 2 - 1, body, carry)
    last = jnp.int32(n_steps - 1)
    _, alpha_o = softmax(last, m, s_odd, p_odd)
    values(last - 1, alpha_e, p_even, False)
    values(last, alpha_o, p_odd, True)


def _mla(q, k, v, tq=512, tk=512):
    bsz, nh, s, w = q.shape
    return pl.pallas_call(
        functools.partial(_mla_kernel, tk=tk),
        grid=(bsz, s // tq),
        in_specs=[
            pl.BlockSpec((None, nh, tq, w), lambda b, i: (b, 0, i, 0)),
            pl.BlockSpec((None, nh, s, w), lambda b, i: (b, 0, 0, 0)),
            pl.BlockSpec((None, v.shape[1], s, w), lambda b, i: (b, 0, 0, 0)),
        ],
        out_specs=pl.BlockSpec((None, v.shape[1], tq, w), lambda b, i: (b, 0, i, 0)),
        out_shape=jax.ShapeDtypeStruct(v.shape, BF16),
        scratch_shapes=[pltpu.VMEM((tq, tk), F32), pltpu.VMEM((tq, tk), F32),
                        pltpu.VMEM((tq, tk), BF16), pltpu.VMEM((tq, tk), BF16),
                        pltpu.VMEM((tq, 2 * LANES), F32)],
        compiler_params=pltpu.CompilerParams(
            dimension_semantics=("arbitrary", "arbitrary"), vmem_limit_bytes=VMEM_LIMIT),
        name="mla",
    )(q, k, v)


def _dil_kernel(q_ref, k_ref, v_ref, o_ref, lse_ref, *, n_side, win):
    tq = q_ref.shape[0]
    seq = k_ref.shape[0]
    qs = pl.program_id(1) * tq
    if win < seq:
        ks = pl.multiple_of(jnp.clip(qs - n_side, 0, seq - win), n_side)
    else:
        ks = 0
    q = q_ref[...]
    kw = k_ref[pl.ds(ks, win), :]
    vw = v_ref[pl.ds(ks, win), :]
    qpos = qs + lax.broadcasted_iota(jnp.int32, (tq, 1), 0)
    kpos = ks + lax.broadcasted_iota(jnp.int32, (1, win), 1)
    valid = jnp.abs(kpos - qpos) <= n_side
    lane_head = lax.broadcasted_iota(jnp.int32, (1, DIL_GROUP_W), 1) // DIL_HEAD_DIM
    o = jnp.zeros((tq, DIL_GROUP_W), F32)
    lse = jnp.zeros((tq, DIL_GROUP_W), F32)
    for h in range(DIL_HEADS):
        hm = lane_head == h
        qh = jnp.where(hm, q, jnp.zeros_like(q))
        s = jnp.where(valid, _dot_nt(qh, kw), NEG_INF)
        m = jnp.max(s, axis=-1, keepdims=True)
        p = jnp.exp(s - m)
        l = jnp.sum(p, axis=-1, keepdims=True)
        pv = _dot(p.astype(BF16), vw)
        o = jnp.where(hm, pv / l, o)
        lse = jnp.where(hm, m + jnp.log(l), lse)
    o_ref[...] = o.astype(BF16)
    lse_ref[...] = lse


def _dilated(q, k, v, n_side, tq=256):
    nseq, seq, w = q.shape
    win = min(seq, tq + 2 * n_side)
    qspec = pl.BlockSpec((None, tq, w), lambda b, i: (b, i, 0))
    kvspec = pl.BlockSpec((None, seq, w), lambda b, i: (b, 0, 0))
    return pl.pallas_call(
        functools.partial(_dil_kernel, n_side=n_side, win=win),
        grid=(nseq, seq // tq),
        in_specs=[qspec, kvspec, kvspec],
        out_specs=(qspec, qspec),
        out_shape=(jax.ShapeDtypeStruct(q.shape, BF16), jax.ShapeDtypeStruct(q.shape, F32)),
        compiler_params=pltpu.CompilerParams(
            dimension_semantics=("arbitrary", "arbitrary"), vmem_limit_bytes=VMEM_LIMIT),
        name="dilated",
    )(q, k, v)


def _post_kernel(x_ref, shift_ref, scale_ref, gate_ref, gpre_ref, gpost_ref, wgate_ref,
                 omla_ref, o0_ref, o1_ref, o2_ref, l0_ref, l1_ref, l2_ref,
                 womla_ref, wodil_ref, wout_ref, out_ref):
    x = x_ref[...]
    u = (_rms(x, gpre_ref[...]) * (1.0 + scale_ref[...]) + shift_ref[...]).astype(BF16)
    gate_a = jax.nn.sigmoid(_dot(u, wgate_ref[:, :D_MODEL]))
    gate_b = jax.nn.sigmoid(_dot(u, wgate_ref[:, D_MODEL:]))
    l0, l1, l2 = l0_ref[...], l1_ref[...], l2_ref[...]
    mx = jnp.maximum(jnp.maximum(l0, l1), l2)
    e0, e1, e2 = jnp.exp(l0 - mx), jnp.exp(l1 - mx), jnp.exp(l2 - mx)
    inv = 1.0 / (e0 + e1 + e2)
    o_dil = ((e0 * inv) * o0_ref[...].astype(F32) + (e1 * inv) * o1_ref[...].astype(F32)
             + (e2 * inv) * o2_ref[...].astype(F32))
    o_mla = jnp.concatenate([omla_ref[p] for p in range(MLA_PAIRS)], axis=-1)
    merged = (gate_a * _dot(o_mla, womla_ref[...])
              + gate_b * _dot(o_dil.astype(BF16), wodil_ref[...]))
    y = _dot(merged.astype(BF16), wout_ref[...])
    out_ref[...] = x + gate_ref[...] * _rms(y, gpost_ref[...])


def _post(x, shift, scale, gate, g_pre, g_post, w_gates, o_mla, o_dil, lse_dil,
          w_o_mla, w_o_dil, w_out, tm=512):
    bsz, s, d = x.shape
    tok = lambda n: pl.BlockSpec((None, tm, n), lambda b, i: (b, i, 0))
    mod = pl.BlockSpec((None, 1, d), lambda b, i: (b, 0, 0))
    gw = DIL_GROUP_W
    return pl.pallas_call(
        _post_kernel,
        grid=(bsz, s // tm),
        in_specs=[tok(d), mod, mod, mod, _const_spec((1, d)), _const_spec((1, d)),
                  _const_spec(w_gates.shape),
                  pl.BlockSpec((None, MLA_PAIRS, tm, LANES), lambda b, i: (b, 0, i, 0)),
                  tok(gw), tok(gw), tok(gw), tok(gw), tok(gw), tok(gw),
                  _const_spec(w_o_mla.shape), _const_spec(w_o_dil.shape),
                  _const_spec(w_out.shape)],
        out_specs=tok(d),
        out_shape=jax.ShapeDtypeStruct(x.shape, F32),
        compiler_params=pltpu.CompilerParams(
            dimension_semantics=("arbitrary", "arbitrary"), vmem_limit_bytes=VMEM_LIMIT),
        name="post",
    )(x, shift, scale, gate, g_pre, g_post, w_gates, o_mla, *o_dil, *lse_dil,
      w_o_mla, w_o_dil, w_out)


def _tables():
    t = np.zeros((N_TABLE_ROWS, LANES), np.float32)
    scale = np.float32((MLA_NOPE + MLA_ROPE) ** -0.5 * LOG2E)
    half = MLA_ROPE // 2
    f_m = MLA_THETA ** (-np.arange(0, MLA_ROPE, 2, dtype=np.float32) / MLA_ROPE)
    t[ROW_INVF_Q, MLA_NOPE:] = np.tile(f_m, 4)
    t[ROW_MC_Q, :MLA_NOPE + MLA_ROPE] = scale
    t[ROW_MS_Q, MLA_NOPE + MLA_ROPE:MLA_NOPE + MLA_ROPE + half] = -scale
    t[ROW_MS_Q, MLA_NOPE + MLA_ROPE + half:] = scale
    t[ROW_INVF_K, :2 * MLA_ROPE] = np.tile(f_m, 4)
    t[ROW_MC_K, :MLA_ROPE] = 1.0
    t[ROW_MS_K, MLA_ROPE:MLA_ROPE + half] = -1.0
    t[ROW_MS_K, MLA_ROPE + half:2 * MLA_ROPE] = 1.0
    f_d = ROPE_THETA ** (-np.arange(0, ROPE_PART, 2, dtype=np.float32) / ROPE_PART)
    hp = ROPE_PART // 2
    for off in (0, DIL_HEAD_DIM):
        t[ROW_INVF_D, off:off + ROPE_PART] = np.tile(f_d, 2)
        t[ROW_M2_D, off:off + hp] = -1.0
        t[ROW_M1_D, off + hp:off + ROPE_PART] = 1.0
    return jnp.asarray(t)


def _layout_weights(w_in, w_uq, w_ukv):
    o_ckv = MLA_Q_LORA
    o_kr = o_ckv + MLA_KV_LORA
    o_qkv = o_kr + MLA_ROPE
    o_ga = o_qkv + DIL_QKV
    half = MLA_ROPE // 2
    k_raw = w_in[:, o_kr:o_qkv]
    k_swap = jnp.concatenate([k_raw[:, half:], k_raw[:, :half]], axis=1)
    pad = jnp.zeros((D_MODEL, 2 * MLA_Q_LORA - MLA_Q_LORA - MLA_KV_LORA - 2 * MLA_ROPE), w_in.dtype)
    w_in2 = jnp.concatenate(
        [w_in[:, :o_kr], k_raw, k_swap, pad, w_in[:, o_qkv:o_ga]], axis=1).astype(BF16)
    w_gates = w_in[:, o_ga:].astype(BF16)

    wq = w_uq.reshape(MLA_Q_LORA, MLA_HEADS, MLA_NOPE + MLA_ROPE)
    nope, t1, t2 = wq[..., :MLA_NOPE], wq[..., MLA_NOPE:MLA_NOPE + half], wq[..., MLA_NOPE + half:]
    w_uq2 = jnp.concatenate([nope, t1, t2, t2, t1], axis=-1).reshape(
        MLA_Q_LORA, MLA_HEADS * MLA_PAD).astype(BF16)

    wkv = w_ukv.reshape(MLA_KV_LORA, MLA_HEADS, MLA_NOPE + MLA_V)
    k_cols = jnp.concatenate(
        [wkv[..., :MLA_NOPE], jnp.zeros((MLA_KV_LORA, MLA_HEADS, MLA_PAD - MLA_NOPE), w_ukv.dtype)],
        axis=-1).reshape(MLA_KV_LORA, MLA_HEADS * MLA_PAD)
    v_cols = wkv[..., MLA_NOPE:].reshape(MLA_KV_LORA, MLA_HEADS * MLA_V)
    place = np.zeros((MLA_ROPE, MLA_HEADS, MLA_PAD), np.float32)
    for i in range(MLA_ROPE):
        place[i, :, MLA_NOPE + i] = 1.0
        place[i, :, MLA_NOPE + MLA_ROPE + i] = 1.0
    place = jnp.asarray(place.reshape(MLA_ROPE, MLA_HEADS * MLA_PAD))
    n_lhs = 2 * MLA_Q_LORA - MLA_Q_LORA
    k_rows = jnp.concatenate(
        [k_cols, place, place,
         jnp.zeros((n_lhs - MLA_KV_LORA - 2 * MLA_ROPE, MLA_HEADS * MLA_PAD), F32)], axis=0)
    v_rows = jnp.concatenate(
        [v_cols, jnp.zeros((n_lhs - MLA_KV_LORA, MLA_HEADS * MLA_V), F32)], axis=0)
    w_kv = jnp.concatenate([k_rows, v_rows], axis=1).astype(BF16)
    return w_in2, w_gates, w_uq2, w_kv


def _ff_weights(w_gate, w_up, w_down):
    wg = w_gate.astype(BF16).reshape(D_MODEL, N_FF_CHUNKS, FF_CHUNK).transpose(1, 0, 2)
    wu = w_up.astype(BF16).reshape(D_MODEL, N_FF_CHUNKS, FF_CHUNK).transpose(1, 0, 2)
    wd = w_down.astype(BF16).reshape(N_FF_CHUNKS, FF_CHUNK, D_MODEL)
    return wg, wu, wd


def _stride_split(t, dil):
    bsz, s, w = t.shape
    return t.reshape(bsz, s // dil, dil, w).transpose(0, 2, 1, 3).reshape(bsz * dil, s // dil, w)


def _stride_merge(t, dil, bsz):
    _, seq, w = t.shape
    return t.reshape(bsz, dil, seq, w).transpose(0, 2, 1, 3).reshape(bsz, seq * dil, w)


def kernel(x, c, positions, w_ada, b_ada, g_pre_ff1, w_gate1, w_up1, w_down1, g_post_ff1, g_pre_mix, w_in, g_cq, w_uq, g_ckv, w_ukv, w_o_mla, w_o_dil, w_out, g_post_mix, g_pre_ff2, w_gate2, w_up2, w_down2, g_post_ff2):
    bsz, s, d = x.shape
    depth = w_ada.shape[0]
    pos = positions.astype(F32).reshape(bsz, s, 1)
    tables = _tables()
    for l in range(depth):
        ada = _ada(c, w_ada[l], b_ada[l]).reshape(bsz, 3, 3, 1, d)
        mod = lambda i, j: ada[:, i, j]

        x = _ffn(x, mod(0, 0), mod(0, 1), mod(0, 2), g_pre_ff1[l][None], g_post_ff1[l][None],
                 *_ff_weights(w_gate1[l], w_up1[l], w_down1[l]))

        w_in2, w_gates, w_uq2, w_kv = _layout_weights(w_in[l], w_uq[l], w_ukv[l])
        q_m, k_m, v_m, dqkv = _proj(x, pos, mod(1, 0), mod(1, 1), g_pre_mix[l][None], tables,
                                    w_in2, g_cq[l][None], w_uq2, g_ckv[l][None], w_kv)
        o_mla = _mla(q_m, k_m, v_m)

        outs, lses = [], []
        gq = N_DIL_GROUPS * DIL_GROUP_W
        for g, (window, dil) in enumerate(DIL_PATTERNS):
            parts = [_stride_split(dqkv[:, :, w * gq + g * DIL_GROUP_W: w * gq + (g + 1) * DIL_GROUP_W], dil)
                     for w in range(3)]
            o_g, lse_g = _dilated(*parts, n_side=window // (2 * dil))
            outs.append(_stride_merge(o_g, dil, bsz))
            lses.append(_stride_merge(lse_g, dil, bsz))

        x = _post(x, mod(1, 0), mod(1, 1), mod(1, 2), g_pre_mix[l][None], g_post_mix[l][None],
                  w_gates, o_mla, outs, lses,
                  w_o_mla[l].astype(BF16), w_o_dil[l].astype(BF16), w_out[l].astype(BF16))

        x = _ffn(x, mod(2, 0), mod(2, 1), mod(2, 2), g_pre_ff2[l][None], g_post_ff2[l][None],
                 *_ff_weights(w_gate2[l], w_up2[l], w_down2[l]))
    return x
```

```python
import functools

import numpy as np
import jax
import jax.numpy as jnp
from jax import lax
from jax.experimental import pallas as pl
from jax.experimental.pallas import tpu as pltpu

F32 = jnp.float32
BF16 = jnp.bfloat16

D_MODEL = 1024
D_FF = 2816
FFN_RES = 0.5
MLA_HEADS = 8
MLA_Q_LORA = 256
MLA_KV_LORA = 128
MLA_NOPE = 64
MLA_ROPE = 32
MLA_V = 64
MLA_THETA = 10000.0
DIL_PATTERNS = ((128, 1), (512, 4), (2048, 16))
N_DIL_GROUPS = 3
DIL_HEADS = 4
DIL_HEAD_DIM = 64
DIL_GROUP_W = DIL_HEADS * DIL_HEAD_DIM
DIL_QKV = 3 * N_DIL_GROUPS * DIL_GROUP_W
ROPE_THETA = 500000.0
ROPE_PART = DIL_HEAD_DIM // 4
NORM_EPS = 1e-6
NEG_INF = -1e30

LANES = 128
MXU_N = 256
FF_CHUNK = MXU_N
N_FF_CHUNKS = D_FF // FF_CHUNK
MLA_PAD = LANES
MLA_PAIRS = MLA_HEADS * MLA_V // LANES
LOG2E = float(np.log2(np.e))
VMEM_LIMIT = 56 * 1024 * 1024

ROW_INVF_Q, ROW_MC_Q, ROW_MS_Q, ROW_INVF_K, ROW_MC_K, ROW_MS_K, ROW_INVF_D, ROW_M1_D, ROW_M2_D = range(9)
N_TABLE_ROWS = 16


def _rms(x, g):
    return x * lax.rsqrt(jnp.mean(x * x, axis=-1, keepdims=True) + NORM_EPS) * g


def _silu(x):
    return x * jax.nn.sigmoid(x)


def _dot(a, b):
    return jnp.dot(a, b, preferred_element_type=F32)


def _dot_nt(a, b):
    return lax.dot_general(a, b, (((1,), (1,)), ((), ())), preferred_element_type=F32)


def _const_spec(shape):
    nd = len(shape)
    return pl.BlockSpec(shape, lambda *_: (0,) * nd, pipeline_mode=pl.Buffered(1))


def _ada_kernel(c_ref, w_ref, b_ref, o_ref):
    cond = _silu(c_ref[...])
    o_ref[...] = _dot(cond, w_ref[...]) + b_ref[...]


def _ada(c, w, b):
    bsz, d = c.shape
    n = w.shape[1]
    tn = 1024
    return pl.pallas_call(
        _ada_kernel,
        grid=(n // tn,),
        in_specs=[
            pl.BlockSpec((bsz, d), lambda j: (0, 0)),
            pl.BlockSpec((d, tn), lambda j: (0, j)),
            pl.BlockSpec((1, tn), lambda j: (0, j)),
        ],
        out_specs=pl.BlockSpec((bsz, tn), lambda j: (0, j)),
        out_shape=jax.ShapeDtypeStruct((bsz, n), F32),
        compiler_params=pltpu.CompilerParams(dimension_semantics=("arbitrary",)),
        name="ada",
    )(c, w, b.reshape(1, n))


def _ffn_kernel(x_ref, shift_ref, scale_ref, gate_ref, gpre_ref, gpost_ref,
                wg_ref, wu_ref, wd_ref, o_ref):
    x = x_ref[...]
    h = _rms(x, gpre_ref[...]) * (1.0 + scale_ref[...]) + shift_ref[...]
    hb = h.astype(BF16)
    acc = None
    for c in range(N_FF_CHUNKS):
        cols = slice(c * FF_CHUNK, (c + 1) * FF_CHUNK)
        g = _dot(hb, wg_ref[:, cols])
        u = _dot(hb, wu_ref[:, cols])
        a = (_silu(g) * u).astype(BF16)
        d = _dot(a, wd_ref[cols, :])
        acc = d if acc is None else acc + d
    y = _rms(acc, gpost_ref[...])
    o_ref[...] = x + (FFN_RES * gate_ref[...]) * y


def _ffn(x, shift, scale, gate, g_pre, g_post, wg, wu, wd, tm=512):
    bsz, s, d = x.shape
    tok = pl.BlockSpec((None, tm, d), lambda b, i: (b, i, 0))
    mod = pl.BlockSpec((None, 1, d), lambda b, i: (b, 0, 0))
    return pl.pallas_call(
        _ffn_kernel,
        grid=(bsz, s // tm),
        in_specs=[tok, mod, mod, mod, _const_spec((1, d)), _const_spec((1, d)),
                  _const_spec(wg.shape), _const_spec(wu.shape), _const_spec(wd.shape)],
        out_specs=tok,
        out_shape=jax.ShapeDtypeStruct(x.shape, F32),
        compiler_params=pltpu.CompilerParams(
            dimension_semantics=("arbitrary", "arbitrary"), vmem_limit_bytes=VMEM_LIMIT),
        name="ffn",
    )(x, shift, scale, gate, g_pre, g_post, wg, wu, wd)


def _proj_kernel(x_ref, pos_ref, shift_ref, scale_ref, gpre_ref, tab_ref, win_ref,
                 gcq_ref, wuq_ref, gckv_ref, wkv_ref,
                 q_ref, k_ref, v_ref, d0_ref, d1_ref, d2_ref, split_scr):
    d_refs = (d0_ref, d1_ref, d2_ref)
    x = x_ref[...]
    u = (_rms(x, gpre_ref[...]) * (1.0 + scale_ref[...]) + shift_ref[...]).astype(BF16)
    pos = pos_ref[...]

    def row(r):
        return tab_ref[r:r + 1, :]

    c_q = _dot(u, win_ref[:, 0:MLA_Q_LORA])
    q_pad = _dot(_rms(c_q, gcq_ref[...]).astype(BF16), wuq_ref[...])
    ang_q = pos * row(ROW_INVF_Q)
    t_q = jnp.cos(ang_q) * row(ROW_MC_Q) + jnp.sin(ang_q) * row(ROW_MS_Q)
    for h in range(MLA_HEADS):
        q_ref[h] = (q_pad[:, h * MLA_PAD:(h + 1) * MLA_PAD] * t_q).astype(BF16)

    pb = _dot(u, win_ref[:, MLA_Q_LORA:2 * MLA_Q_LORA])
    c_kv = pb[:, :MLA_KV_LORA]
    r_kv = lax.rsqrt(jnp.mean(c_kv * c_kv, axis=-1, keepdims=True) + NORM_EPS)
    ang_k = pos * row(ROW_INVF_K)
    t_k = jnp.cos(ang_k) * row(ROW_MC_K) + jnp.sin(ang_k) * row(ROW_MS_K)
    lhs = jnp.concatenate([c_kv * r_kv * gckv_ref[...], pb[:, MLA_KV_LORA:] * t_k], axis=-1)
    kv = _dot(lhs.astype(BF16), wkv_ref[...])
    for h in range(MLA_HEADS):
        k_ref[h] = kv[:, h * MLA_PAD:(h + 1) * MLA_PAD].astype(BF16)
    v_off = MLA_HEADS * MLA_PAD
    for p in range(MLA_PAIRS):
        v_ref[p] = kv[:, v_off + p * LANES: v_off + (p + 1) * LANES].astype(BF16)

    ang_d = pos * row(ROW_INVF_D)
    cd = jnp.cos(ang_d)
    sd = jnp.sin(ang_d)
    s1 = sd * row(ROW_M1_D)
    s2 = sd * row(ROW_M2_D)
    q_scale = DIL_HEAD_DIM ** -0.5
    base = 2 * MLA_Q_LORA
    n_chunks = DIL_QKV // MXU_N
    rope_chunks = 2 * N_DIL_GROUPS * DIL_GROUP_W // MXU_N
    q_chunks = N_DIL_GROUPS * DIL_GROUP_W // MXU_N
    tm = x.shape[0]
    for j in range(n_chunks):
        which, g = divmod(j, N_DIL_GROUPS)
        dil = DIL_PATTERNS[g][1]
        d_ref = d_refs[g]
        cols = slice(which * DIL_GROUP_W, (which + 1) * DIL_GROUP_W)
        pc = _dot(u, win_ref[:, base + j * MXU_N: base + (j + 1) * MXU_N])
        for half in range(MXU_N // LANES):
            xs = pc[:, half * LANES:(half + 1) * LANES]
            if j < rope_chunks:
                xs = (xs * cd + pltpu.roll(xs, ROPE_PART // 2, 1) * s1
                      + pltpu.roll(xs, LANES - ROPE_PART // 2, 1) * s2)
                if j < q_chunks:
                    xs = xs * q_scale
            lo = cols.start + half * LANES
            if dil == 1:
                d_ref[0, :, lo:lo + LANES] = xs.astype(BF16)
            else:
                split_scr[half] = xs
                for r in range(dil):
                    d_ref[r, :, lo:lo + LANES] = (
                        split_scr[half, pl.ds(r, tm // dil, stride=dil), :].astype(BF16))


def _proj(x, pos, shift, scale, g_pre, tables, w_in2, g_cq, w_uq2, g_ckv, w_kv, tm=512):
    bsz, s, d = x.shape
    tok = lambda n: pl.BlockSpec((None, tm, n), lambda b, i: (b, i, 0))
    mod = pl.BlockSpec((None, 1, d), lambda b, i: (b, 0, 0))
    heads = lambda n: pl.BlockSpec((None, n, tm, LANES), lambda b, i: (b, 0, i, 0))
    out_shapes = (
        jax.ShapeDtypeStruct((bsz, MLA_HEADS, s, MLA_PAD), BF16),
        jax.ShapeDtypeStruct((bsz, MLA_HEADS, s, MLA_PAD), BF16),
        jax.ShapeDtypeStruct((bsz, MLA_PAIRS, s, LANES), BF16),
    ) + tuple(jax.ShapeDtypeStruct((bsz, dil, s // dil, 3 * DIL_GROUP_W), BF16)
              for _, dil in DIL_PATTERNS)
    split = lambda dil: pl.BlockSpec((None, dil, tm // dil, 3 * DIL_GROUP_W),
                                     lambda b, i: (b, 0, i, 0))
    return pl.pallas_call(
        _proj_kernel,
        grid=(bsz, s // tm),
        in_specs=[tok(d), tok(1), mod, mod, _const_spec((1, d)), _const_spec(tables.shape),
                  _const_spec(w_in2.shape), _const_spec(g_cq.shape), _const_spec(w_uq2.shape),
                  _const_spec(g_ckv.shape), _const_spec(w_kv.shape)],
        out_specs=(heads(MLA_HEADS), heads(MLA_HEADS), heads(MLA_PAIRS))
        + tuple(split(dil) for _, dil in DIL_PATTERNS),
        out_shape=out_shapes,
        scratch_shapes=[pltpu.VMEM((MXU_N // LANES, tm, LANES), F32)],
        compiler_params=pltpu.CompilerParams(
            dimension_semantics=("arbitrary", "arbitrary"), vmem_limit_bytes=VMEM_LIMIT),
        name="proj",
    )(x, pos, shift, scale, g_pre, tables, w_in2, g_cq, w_uq2, g_ckv, w_kv)


def _mla_kernel(q_ref, k_ref, v_ref, o_ref, s_even, s_odd, p_even, p_odd, acc_scr, *, tk):
    tq = q_ref.shape[1]
    n_kv = k_ref.shape[1] // tk
    kv_bits = n_kv.bit_length() - 1
    assert n_kv == 1 << kv_bits and n_kv % 2 == 0
    n_steps = MLA_HEADS * n_kv
    lane = lax.broadcasted_iota(jnp.int32, (1, LANES), 1)
    ones = jnp.ones((tk, LANES), BF16)

    def split(t):
        return lax.shift_right_logical(t, kv_bits), lax.bitwise_and(t, n_kv - 1)

    def scores(t, s_scr):
        h, j = split(t)
        rows = pl.ds(pl.multiple_of(j * tk, tk), tk)
        s_scr[...] = _dot_nt(q_ref[h], k_ref[h, rows, :])

    def softmax(t, m, s_scr, p_scr):
        _, j = split(t)
        s = s_scr[...]
        m_prev = jnp.where(j == 0, NEG_INF, m)
        m_new = jnp.maximum(m_prev, jnp.max(s, axis=-1, keepdims=True))
        p_scr[...] = jnp.exp2(s - m_new).astype(BF16)
        return m_new, jnp.exp2(m_prev - m_new)

    def values(t, alpha, p_scr, may_finish):
        h, j = split(t)
        pair = lax.shift_right_logical(h, 1)
        rows = pl.ds(pl.multiple_of(j * tk, tk), tk)
        rhs = jnp.concatenate([v_ref[pair, rows, :], ones], axis=1)
        acc = alpha * acc_scr[...] + _dot(p_scr[...], rhs)
        acc_scr[...] = acc
        if may_finish:
            @pl.when(j == n_kv - 1)
            def _():
                o = acc[:, :LANES] / acc[:, LANES:]
                odd = lax.bitwise_and(h, 1) == 1
                keep = jnp.logical_and(odd, lane < MLA_V)
                o_ref[pair] = jnp.where(keep, o_ref[pair].astype(F32), o).astype(BF16)

    def body(u, carry):
        m, alpha_e = carry
        t = 2 * u + 1
        scores(t + 1, s_even)
        m, alpha_o = softmax(t, m, s_odd, p_odd)
        values(t - 1, alpha_e, p_even, False)
        scores(t + 2, s_odd)
        m, alpha_e = softmax(t + 1, m, s_even, p_even)
        values(t, alpha_o, p_odd, True)
        return m, alpha_e

    o_ref[...] = jnp.zeros(o_ref.shape, BF16)
    acc_scr[...] = jnp.zeros(acc_scr.shape, F32)
    scores(jnp.int32(0), s_even)
    scores(jnp.int32(1), s_odd)
    carry = softmax(jnp.int32(0), jnp.full((tq, 1), NEG_INF, F32), s_even, p_even)
    m, alpha_e = lax.fori_loop(0, n_steps // 2 - 1, body, carry)
    last = jnp.int32(n_steps - 1)
    _, alpha_o = softmax(last, m, s_odd, p_odd)
    values(last - 1, alpha_e, p_even, False)
    values(last, alpha_o, p_odd, True)


def _mla(q, k, v, tq=512, tk=512):
    bsz, nh, s, w = q.shape
    return pl.pallas_call(
        functools.partial(_mla_kernel, tk=tk),
        grid=(bsz, s // tq),
        in_specs=[
            pl.BlockSpec((None, nh, tq, w), lambda b, i: (b, 0, i, 0)),
            pl.BlockSpec((None, nh, s, w), lambda b, i: (b, 0, 0, 0)),
            pl.BlockSpec((None, v.shape[1], s, w), lambda b, i: (b, 0, 0, 0)),
        ],
        out_specs=pl.BlockSpec((None, v.shape[1], tq, w), lambda b, i: (b, 0, i, 0)),
        out_shape=jax.ShapeDtypeStruct(v.shape, BF16),
        scratch_shapes=[pltpu.VMEM((tq, tk), F32), pltpu.VMEM((tq, tk), F32),
                        pltpu.VMEM((tq, tk), BF16), pltpu.VMEM((tq, tk), BF16),
                        pltpu.VMEM((tq, 2 * LANES), F32)],
        compiler_params=pltpu.CompilerParams(
            dimension_semantics=("arbitrary", "arbitrary"), vmem_limit_bytes=VMEM_LIMIT),
        name="mla",
    )(q, k, v)


def _dil_kernel(q_ref, k_ref, v_ref, o_ref, lse_ref, *, n_side, win):
    tq = q_ref.shape[0]
    seq = k_ref.shape[0]
    qs = pl.program_id(2) * tq
    if win < seq:
        ks = pl.multiple_of(jnp.clip(qs - n_side, 0, seq - win), n_side)
    else:
        ks = 0
    q = q_ref[...]
    kw = k_ref[pl.ds(ks, win), :]
    vw = v_ref[pl.ds(ks, win), :]
    qpos = qs + lax.broadcasted_iota(jnp.int32, (tq, 1), 0)
    kpos = ks + lax.broadcasted_iota(jnp.int32, (1, win), 1)
    valid = jnp.abs(kpos - qpos) <= n_side
    lane_head = lax.broadcasted_iota(jnp.int32, (1, DIL_GROUP_W), 1) // DIL_HEAD_DIM
    o = jnp.zeros((tq, DIL_GROUP_W), F32)
    lse = jnp.zeros((tq, DIL_GROUP_W), F32)
    for h in range(DIL_HEADS):
        hm = lane_head == h
        qh = jnp.where(hm, q, jnp.zeros_like(q))
        s = jnp.where(valid, _dot_nt(qh, kw), NEG_INF)
        m = jnp.max(s, axis=-1, keepdims=True)
        p = jnp.exp(s - m)
        l = jnp.sum(p, axis=-1, keepdims=True)
        pv = _dot(p.astype(BF16), vw)
        o = jnp.where(hm, pv / l, o)
        lse = jnp.where(hm, m + jnp.log(l), lse)
    o_ref[...] = o.astype(BF16)
    lse_ref[...] = lse


def _dilated(qkv, n_side, tq=256):
    bsz, dil, seq, _ = qkv.shape
    w = DIL_GROUP_W
    win = min(seq, tq + 2 * n_side)
    qspec = pl.BlockSpec((None, None, tq, w), lambda b, r, i: (b, r, i, 0))
    kspec = pl.BlockSpec((None, None, seq, w), lambda b, r, i: (b, r, 0, 1))
    vspec = pl.BlockSpec((None, None, seq, w), lambda b, r, i: (b, r, 0, 2))
    out = (bsz, dil, seq, w)
    return pl.pallas_call(
        functools.partial(_dil_kernel, n_side=n_side, win=win),
        grid=(bsz, dil, seq // tq),
        in_specs=[qspec, kspec, vspec],
        out_specs=(qspec, qspec),
        out_shape=(jax.ShapeDtypeStruct(out, BF16), jax.ShapeDtypeStruct(out, F32)),
        compiler_params=pltpu.CompilerParams(
            dimension_semantics=("arbitrary",) * 3, vmem_limit_bytes=VMEM_LIMIT),
        name="dilated",
    )(qkv, qkv, qkv)


def _post_kernel(x_ref, shift_ref, scale_ref, gate_ref, gpre_ref, gpost_ref, wgate_ref,
                 omla_ref, o0_ref, o1_ref, o2_ref, l0_ref, l1_ref, l2_ref,
                 womla_ref, wodil_ref, wout_ref, out_ref, merge_scr):
    x = x_ref[...]
    tm = x.shape[0]
    u = (_rms(x, gpre_ref[...]) * (1.0 + scale_ref[...]) + shift_ref[...]).astype(BF16)
    gate_a = jax.nn.sigmoid(_dot(u, wgate_ref[:, :D_MODEL]))
    gate_b = jax.nn.sigmoid(_dot(u, wgate_ref[:, D_MODEL:]))

    def natural(ref, slot):
        dil = ref.shape[0]
        if dil == 1:
            return ref[0].astype(F32)
        halves = DIL_GROUP_W // LANES
        for r in range(dil):
            blk = ref[r].astype(F32)
            for hf in range(halves):
                merge_scr[slot * halves + hf, pl.ds(r, tm // dil, stride=dil), :] = (
                    blk[:, hf * LANES:(hf + 1) * LANES])
        return jnp.concatenate([merge_scr[slot * halves + hf] for hf in range(halves)], axis=-1)

    l0, l1, l2 = natural(l0_ref, 0), natural(l1_ref, 1), natural(l2_ref, 2)
    mx = jnp.maximum(jnp.maximum(l0, l1), l2)
    e0, e1, e2 = jnp.exp(l0 - mx), jnp.exp(l1 - mx), jnp.exp(l2 - mx)
    inv = 1.0 / (e0 + e1 + e2)
    o_dil = ((e0 * inv) * natural(o0_ref, 3) + (e1 * inv) * natural(o1_ref, 4)
             + (e2 * inv) * natural(o2_ref, 5))
    o_mla = jnp.concatenate([omla_ref[p] for p in range(MLA_PAIRS)], axis=-1)
    merged = (gate_a * _dot(o_mla, womla_ref[...])
              + gate_b * _dot(o_dil.astype(BF16), wodil_ref[...]))
    y = _dot(merged.astype(BF16), wout_ref[...])
    out_ref[...] = x + gate_ref[...] * _rms(y, gpost_ref[...])


def _post(x, shift, scale, gate, g_pre, g_post, w_gates, o_mla, o_dil, lse_dil,
          w_o_mla, w_o_dil, w_out, tm=512):
    bsz, s, d = x.shape
    tok = lambda n: pl.BlockSpec((None, tm, n), lambda b, i: (b, i, 0))
    mod = pl.BlockSpec((None, 1, d), lambda b, i: (b, 0, 0))
    gw = DIL_GROUP_W
    split = [pl.BlockSpec((None, dil, tm // dil, gw), lambda b, i: (b, 0, i, 0))
             for _, dil in DIL_PATTERNS]
    return pl.pallas_call(
        _post_kernel,
        grid=(bsz, s // tm),
        in_specs=[tok(d), mod, mod, mod, _const_spec((1, d)), _const_spec((1, d)),
                  _const_spec(w_gates.shape),
                  pl.BlockSpec((None, MLA_PAIRS, tm, LANES), lambda b, i: (b, 0, i, 0)),
                  *split, *split,
                  _const_spec(w_o_mla.shape), _const_spec(w_o_dil.shape),
                  _const_spec(w_out.shape)],
        out_specs=tok(d),
        out_shape=jax.ShapeDtypeStruct(x.shape, F32),
        scratch_shapes=[pltpu.VMEM((2 * N_DIL_GROUPS * gw // LANES, tm, LANES), F32)],
        compiler_params=pltpu.CompilerParams(
            dimension_semantics=("arbitrary", "arbitrary"), vmem_limit_bytes=VMEM_LIMIT),
        name="post",
    )(x, shift, scale, gate, g_pre, g_post, w_gates, o_mla, *o_dil, *lse_dil,
      w_o_mla, w_o_dil, w_out)


def _tables():
    t = np.zeros((N_TABLE_ROWS, LANES), np.float32)
    scale = np.float32((MLA_NOPE + MLA_ROPE) ** -0.5 * LOG2E)
    half = MLA_ROPE // 2
    f_m = MLA_THETA ** (-np.arange(0, MLA_ROPE, 2, dtype=np.float32) / MLA_ROPE)
    t[ROW_INVF_Q, MLA_NOPE:] = np.tile(f_m, 4)
    t[ROW_MC_Q, :MLA_NOPE + MLA_ROPE] = scale
    t[ROW_MS_Q, MLA_NOPE + MLA_ROPE:MLA_NOPE + MLA_ROPE + half] = -scale
    t[ROW_MS_Q, MLA_NOPE + MLA_ROPE + half:] = scale
    t[ROW_INVF_K, :2 * MLA_ROPE] = np.tile(f_m, 4)
    t[ROW_MC_K, :MLA_ROPE] = 1.0
    t[ROW_MS_K, MLA_ROPE:MLA_ROPE + half] = -1.0
    t[ROW_MS_K, MLA_ROPE + half:2 * MLA_ROPE] = 1.0
    f_d = ROPE_THETA ** (-np.arange(0, ROPE_PART, 2, dtype=np.float32) / ROPE_PART)
    hp = ROPE_PART // 2
    for off in (0, DIL_HEAD_DIM):
        t[ROW_INVF_D, off:off + ROPE_PART] = np.tile(f_d, 2)
        t[ROW_M2_D, off:off + hp] = -1.0
        t[ROW_M1_D, off + hp:off + ROPE_PART] = 1.0
    return jnp.asarray(t)


def _layout_weights(w_in, w_uq, w_ukv):
    o_ckv = MLA_Q_LORA
    o_kr = o_ckv + MLA_KV_LORA
    o_qkv = o_kr + MLA_ROPE
    o_ga = o_qkv + DIL_QKV
    half = MLA_ROPE // 2
    k_raw = w_in[:, o_kr:o_qkv]
    k_swap = jnp.concatenate([k_raw[:, half:], k_raw[:, :half]], axis=1)
    pad = jnp.zeros((D_MODEL, 2 * MLA_Q_LORA - MLA_Q_LORA - MLA_KV_LORA - 2 * MLA_ROPE), w_in.dtype)
    w_in2 = jnp.concatenate(
        [w_in[:, :o_kr], k_raw, k_swap, pad, w_in[:, o_qkv:o_ga]], axis=1).astype(BF16)
    w_gates = w_in[:, o_ga:].astype(BF16)

    wq = w_uq.reshape(MLA_Q_LORA, MLA_HEADS, MLA_NOPE + MLA_ROPE)
    nope, t1, t2 = wq[..., :MLA_NOPE], wq[..., MLA_NOPE:MLA_NOPE + half], wq[..., MLA_NOPE + half:]
    w_uq2 = jnp.concatenate([nope, t1, t2, t2, t1], axis=-1).reshape(
        MLA_Q_LORA, MLA_HEADS * MLA_PAD).astype(BF16)

    wkv = w_ukv.reshape(MLA_KV_LORA, MLA_HEADS, MLA_NOPE + MLA_V)
    k_cols = jnp.concatenate(
        [wkv[..., :MLA_NOPE], jnp.zeros((MLA_KV_LORA, MLA_HEADS, MLA_PAD - MLA_NOPE), w_ukv.dtype)],
        axis=-1).reshape(MLA_KV_LORA, MLA_HEADS * MLA_PAD)
    v_cols = wkv[..., MLA_NOPE:].reshape(MLA_KV_LORA, MLA_HEADS * MLA_V)
    place = np.zeros((MLA_ROPE, MLA_HEADS, MLA_PAD), np.float32)
    for i in range(MLA_ROPE):
        place[i, :, MLA_NOPE + i] = 1.0
        place[i, :, MLA_NOPE + MLA_ROPE + i] = 1.0
    place = jnp.asarray(place.reshape(MLA_ROPE, MLA_HEADS * MLA_PAD))
    n_lhs = 2 * MLA_Q_LORA - MLA_Q_LORA
    k_rows = jnp.concatenate(
        [k_cols, place, place,
         jnp.zeros((n_lhs - MLA_KV_LORA - 2 * MLA_ROPE, MLA_HEADS * MLA_PAD), F32)], axis=0)
    v_rows = jnp.concatenate(
        [v_cols, jnp.zeros((n_lhs - MLA_KV_LORA, MLA_HEADS * MLA_V), F32)], axis=0)
    w_kv = jnp.concatenate([k_rows, v_rows], axis=1).astype(BF16)
    return w_in2, w_gates, w_uq2, w_kv


def _ff_weights(w_gate, w_up, w_down):
    return w_gate.astype(BF16), w_up.astype(BF16), w_down.astype(BF16)


def kernel(x, c, positions, w_ada, b_ada, g_pre_ff1, w_gate1, w_up1, w_down1, g_post_ff1, g_pre_mix, w_in, g_cq, w_uq, g_ckv, w_ukv, w_o_mla, w_o_dil, w_out, g_post_mix, g_pre_ff2, w_gate2, w_up2, w_down2, g_post_ff2):
    bsz, s, d = x.shape
    depth = w_ada.shape[0]
    pos = positions.astype(F32).reshape(bsz, s, 1)
    tables = _tables()
    for l in range(depth):
        ada = _ada(c, w_ada[l], b_ada[l]).reshape(bsz, 3, 3, 1, d)
        mod = lambda i, j: ada[:, i, j]

        x = _ffn(x, mod(0, 0), mod(0, 1), mod(0, 2), g_pre_ff1[l][None], g_post_ff1[l][None],
                 *_ff_weights(w_gate1[l], w_up1[l], w_down1[l]))

        w_in2, w_gates, w_uq2, w_kv = _layout_weights(w_in[l], w_uq[l], w_ukv[l])
        q_m, k_m, v_m, *dil_qkv = _proj(x, pos, mod(1, 0), mod(1, 1), g_pre_mix[l][None], tables,
                                        w_in2, g_cq[l][None], w_uq2, g_ckv[l][None], w_kv)
        o_mla = _mla(q_m, k_m, v_m)

        outs, lses = [], []
        for qkv_g, (window, dil) in zip(dil_qkv, DIL_PATTERNS):
            o_g, lse_g = _dilated(qkv_g, n_side=window // (2 * dil))
            outs.append(o_g)
            lses.append(lse_g)

        x = _post(x, mod(1, 0), mod(1, 1), mod(1, 2), g_pre_mix[l][None], g_post_mix[l][None],
                  w_gates, o_mla, outs, lses,
                  w_o_mla[l].astype(BF16), w_o_dil[l].astype(BF16), w_out[l].astype(BF16))

        x = _ffn(x, mod(2, 0), mod(2, 1), mod(2, 2), g_pre_ff2[l][None], g_post_ff2[l][None],
                 *_ff_weights(w_gate2[l], w_up2[l], w_down2[l]))
    return x
```

```python
import functools

import numpy as np
import jax
import jax.numpy as jnp
from jax import lax
from jax.experimental import pallas as pl
from jax.experimental.pallas import tpu as pltpu

F32 = jnp.float32
BF16 = jnp.bfloat16

D_MODEL = 1024
D_FF = 2816
FFN_RES = 0.5
MLA_HEADS = 8
MLA_Q_LORA = 256
MLA_KV_LORA = 128
MLA_NOPE = 64
MLA_ROPE = 32
MLA_V = 64
MLA_THETA = 10000.0
DIL_PATTERNS = ((128, 1), (512, 4), (2048, 16))
N_DIL_GROUPS = 3
DIL_HEADS = 4
DIL_HEAD_DIM = 64
DIL_GROUP_W = DIL_HEADS * DIL_HEAD_DIM
DIL_QKV = 3 * N_DIL_GROUPS * DIL_GROUP_W
ROPE_THETA = 500000.0
ROPE_PART = DIL_HEAD_DIM // 4
NORM_EPS = 1e-6
NEG_INF = -1e30

LANES = 128
MXU_N = 256
FF_CHUNK = MXU_N
N_FF_CHUNKS = D_FF // FF_CHUNK
MLA_PAD = LANES
MLA_PAIRS = MLA_HEADS * MLA_V // LANES
LOG2E = float(np.log2(np.e))
VMEM_LIMIT = 56 * 1024 * 1024

ROW_INVF_Q, ROW_MC_Q, ROW_MS_Q, ROW_INVF_K, ROW_MC_K, ROW_MS_K, ROW_INVF_D, ROW_M1_D, ROW_M2_D = range(9)
N_TABLE_ROWS = 16


def _rms(x, g):
    return x * lax.rsqrt(jnp.mean(x * x, axis=-1, keepdims=True) + NORM_EPS) * g


def _silu(x):
    return x * jax.nn.sigmoid(x)


def _dot(a, b):
    return jnp.dot(a, b, preferred_element_type=F32)


def _dot_nt(a, b):
    return lax.dot_general(a, b, (((1,), (1,)), ((), ())), preferred_element_type=F32)


def _const_spec(shape):
    nd = len(shape)
    return pl.BlockSpec(shape, lambda *_: (0,) * nd, pipeline_mode=pl.Buffered(1))


def _ada_kernel(c_ref, w_ref, b_ref, o_ref):
    cond = _silu(c_ref[...])
    o_ref[...] = _dot(cond, w_ref[...]) + b_ref[...]


def _ada(c, w, b):
    bsz, d = c.shape
    n = w.shape[1]
    tn = 1024
    return pl.pallas_call(
        _ada_kernel,
        grid=(n // tn,),
        in_specs=[
            pl.BlockSpec((bsz, d), lambda j: (0, 0)),
            pl.BlockSpec((d, tn), lambda j: (0, j)),
            pl.BlockSpec((1, tn), lambda j: (0, j)),
        ],
        out_specs=pl.BlockSpec((bsz, tn), lambda j: (0, j)),
        out_shape=jax.ShapeDtypeStruct((bsz, n), F32),
        compiler_params=pltpu.CompilerParams(dimension_semantics=("arbitrary",)),
        name="ada",
    )(c, w, b.reshape(1, n))


def _ffn_kernel(x_ref, shift_ref, scale_ref, gate_ref, gpre_ref, gpost_ref,
                wg_ref, wu_ref, wd_ref, o_ref):
    x = x_ref[...]
    h = _rms(x, gpre_ref[...]) * (1.0 + scale_ref[...]) + shift_ref[...]
    hb = h.astype(BF16)
    acc = None
    for c in range(N_FF_CHUNKS):
        cols = slice(c * FF_CHUNK, (c + 1) * FF_CHUNK)
        g = _dot(hb, wg_ref[:, cols])
        u = _dot(hb, wu_ref[:, cols])
        a = (_silu(g) * u).astype(BF16)
        d = _dot(a, wd_ref[cols, :])
        acc = d if acc is None else acc + d
    y = _rms(acc, gpost_ref[...])
    o_ref[...] = x + (FFN_RES * gate_ref[...]) * y


def _ffn(x, shift, scale, gate, g_pre, g_post, wg, wu, wd, tm=512):
    bsz, s, d = x.shape
    tok = pl.BlockSpec((None, tm, d), lambda b, i: (b, i, 0))
    mod = pl.BlockSpec((None, 1, d), lambda b, i: (b, 0, 0))
    return pl.pallas_call(
        _ffn_kernel,
        grid=(bsz, s // tm),
        in_specs=[tok, mod, mod, mod, _const_spec((1, d)), _const_spec((1, d)),
                  _const_spec(wg.shape), _const_spec(wu.shape), _const_spec(wd.shape)],
        out_specs=tok,
        out_shape=jax.ShapeDtypeStruct(x.shape, F32),
        compiler_params=pltpu.CompilerParams(
            dimension_semantics=("arbitrary", "arbitrary"), vmem_limit_bytes=VMEM_LIMIT),
        name="ffn",
    )(x, shift, scale, gate, g_pre, g_post, wg, wu, wd)


def _proj_kernel(x_ref, pos_ref, shift_ref, scale_ref, gpre_ref, tab_ref, win_ref,
                 gcq_ref, wuq_ref, gckv_ref, wkv_ref,
                 q_ref, k_ref, v_ref, d0_ref, d1_ref, d2_ref, split_scr):
    d_refs = (d0_ref, d1_ref, d2_ref)
    x = x_ref[...]
    u = (_rms(x, gpre_ref[...]) * (1.0 + scale_ref[...]) + shift_ref[...]).astype(BF16)
    pos = pos_ref[...]

    def row(r):
        return tab_ref[r:r + 1, :]

    c_q = _dot(u, win_ref[:, 0:MLA_Q_LORA])
    q_pad = _dot(_rms(c_q, gcq_ref[...]).astype(BF16), wuq_ref[...])
    ang_q = pos * row(ROW_INVF_Q)
    t_q = jnp.cos(ang_q) * row(ROW_MC_Q) + jnp.sin(ang_q) * row(ROW_MS_Q)
    for h in range(MLA_HEADS):
        q_ref[h] = (q_pad[:, h * MLA_PAD:(h + 1) * MLA_PAD] * t_q).astype(BF16)

    pb = _dot(u, win_ref[:, MLA_Q_LORA:2 * MLA_Q_LORA])
    c_kv = pb[:, :MLA_KV_LORA]
    r_kv = lax.rsqrt(jnp.mean(c_kv * c_kv, axis=-1, keepdims=True) + NORM_EPS)
    ang_k = pos * row(ROW_INVF_K)
    t_k = jnp.cos(ang_k) * row(ROW_MC_K) + jnp.sin(ang_k) * row(ROW_MS_K)
    lhs = jnp.concatenate([c_kv * r_kv * gckv_ref[...], pb[:, MLA_KV_LORA:] * t_k], axis=-1)
    kv = _dot(lhs.astype(BF16), wkv_ref[...])
    for h in range(MLA_HEADS):
        k_ref[h] = kv[:, h * MLA_PAD:(h + 1) * MLA_PAD].astype(BF16)
    v_off = MLA_HEADS * MLA_PAD
    for p in range(MLA_PAIRS):
        v_ref[p] = kv[:, v_off + p * LANES: v_off + (p + 1) * LANES].astype(BF16)

    ang_d = pos * row(ROW_INVF_D)
    cd = jnp.cos(ang_d)
    sd = jnp.sin(ang_d)
    s1 = sd * row(ROW_M1_D)
    s2 = sd * row(ROW_M2_D)
    q_scale = DIL_HEAD_DIM ** -0.5
    base = 2 * MLA_Q_LORA
    n_chunks = DIL_QKV // MXU_N
    rope_chunks = 2 * N_DIL_GROUPS * DIL_GROUP_W // MXU_N
    q_chunks = N_DIL_GROUPS * DIL_GROUP_W // MXU_N
    tm = x.shape[0]
    for j in range(n_chunks):
        which, g = divmod(j, N_DIL_GROUPS)
        dil = DIL_PATTERNS[g][1]
        d_ref = d_refs[g]
        cols = slice(which * DIL_GROUP_W, (which + 1) * DIL_GROUP_W)
        pc = _dot(u, win_ref[:, base + j * MXU_N: base + (j + 1) * MXU_N])
        for half in range(MXU_N // LANES):
            xs = pc[:, half * LANES:(half + 1) * LANES]
            if j < rope_chunks:
                xs = (xs * cd + pltpu.roll(xs, ROPE_PART // 2, 1) * s1
                      + pltpu.roll(xs, LANES - ROPE_PART // 2, 1) * s2)
                if j < q_chunks:
                    xs = xs * q_scale
            lo = cols.start + half * LANES
            if dil == 1:
                d_ref[0, :, lo:lo + LANES] = xs.astype(BF16)
            else:
                split_scr[half] = xs
                for r in range(dil):
                    d_ref[r, :, lo:lo + LANES] = (
                        split_scr[half, pl.ds(r, tm // dil, stride=dil), :].astype(BF16))


def _proj(x, pos, shift, scale, g_pre, tables, w_in2, g_cq, w_uq2, g_ckv, w_kv, tm=512):
    bsz, s, d = x.shape
    tok = lambda n: pl.BlockSpec((None, tm, n), lambda b, i: (b, i, 0))
    mod = pl.BlockSpec((None, 1, d), lambda b, i: (b, 0, 0))
    heads = lambda n: pl.BlockSpec((None, n, tm, LANES), lambda b, i: (b, 0, i, 0))
    out_shapes = (
        jax.ShapeDtypeStruct((bsz, MLA_HEADS, s, MLA_PAD), BF16),
        jax.ShapeDtypeStruct((bsz, MLA_HEADS, s, MLA_PAD), BF16),
        jax.ShapeDtypeStruct((bsz, MLA_PAIRS, s, LANES), BF16),
    ) + tuple(jax.ShapeDtypeStruct((bsz, dil, s // dil, 3 * DIL_GROUP_W), BF16)
              for _, dil in DIL_PATTERNS)
    split = lambda dil: pl.BlockSpec((None, dil, tm // dil, 3 * DIL_GROUP_W),
                                     lambda b, i: (b, 0, i, 0))
    return pl.pallas_call(
        _proj_kernel,
        grid=(bsz, s // tm),
        in_specs=[tok(d), tok(1), mod, mod, _const_spec((1, d)), _const_spec(tables.shape),
                  _const_spec(w_in2.shape), _const_spec(g_cq.shape), _const_spec(w_uq2.shape),
                  _const_spec(g_ckv.shape), _const_spec(w_kv.shape)],
        out_specs=(heads(MLA_HEADS), heads(MLA_HEADS), heads(MLA_PAIRS))
        + tuple(split(dil) for _, dil in DIL_PATTERNS),
        out_shape=out_shapes,
        scratch_shapes=[pltpu.VMEM((MXU_N // LANES, tm, LANES), F32)],
        compiler_params=pltpu.CompilerParams(
            dimension_semantics=("arbitrary", "arbitrary"), vmem_limit_bytes=VMEM_LIMIT),
        name="proj",
    )(x, pos, shift, scale, g_pre, tables, w_in2, g_cq, w_uq2, g_ckv, w_kv)


def _mla_kernel(q_ref, k_ref, v_ref, o_ref, s_even, s_odd, p_even, p_odd, acc_scr, *,
                tk, unroll):
    tq = q_ref.shape[1]
    n_kv = k_ref.shape[1] // tk
    kv_bits = n_kv.bit_length() - 1
    assert n_kv == 1 << kv_bits and n_kv % 2 == 0 and unroll % 2 == 0
    n_steps = MLA_HEADS * n_kv
    lane = lax.broadcasted_iota(jnp.int32, (1, LANES), 1)
    ones = jnp.ones((tk, LANES), BF16)

    def split(t):
        return lax.shift_right_logical(t, kv_bits), lax.bitwise_and(t, n_kv - 1)

    def scores(t, s_scr):
        h, j = split(t)
        rows = pl.ds(pl.multiple_of(j * tk, tk), tk)
        s_scr[...] = _dot_nt(q_ref[h], k_ref[h, rows, :])

    def softmax(t, m, s_scr, p_scr):
        _, j = split(t)
        s = s_scr[...]
        m_prev = jnp.where(j == 0, NEG_INF, m)
        m_new = jnp.maximum(m_prev, jnp.max(s, axis=-1, keepdims=True))
        p_scr[...] = jnp.exp2(s - m_new).astype(BF16)
        return m_new, jnp.exp2(m_prev - m_new)

    def values(t, alpha, p_scr):
        h, j = split(t)
        rows = pl.ds(pl.multiple_of(j * tk, tk), tk)
        rhs = jnp.concatenate([v_ref[lax.shift_right_logical(h, 1), rows, :], ones], axis=1)
        acc_scr[h] = alpha * acc_scr[h] + _dot(p_scr[...], rhs)

    s_buf, p_buf = (s_even, s_odd), (p_even, p_odd)

    def time_step(t, par, m, alpha_prev, with_scores=True):
        if with_scores:
            scores(t + 1, s_buf[1 - par])
        m, alpha = softmax(t, m, s_buf[par], p_buf[par])
        values(t - 1, alpha_prev, p_buf[1 - par])
        return m, alpha

    def body(u, carry):
        t0 = unroll * u + 1
        for i in range(unroll):
            carry = time_step(t0 + i, (1 + i) % 2, *carry)
        return carry

    acc_scr[...] = jnp.zeros(acc_scr.shape, F32)
    scores(jnp.int32(0), s_even)
    scores(jnp.int32(1), s_odd)
    carry = softmax(jnp.int32(0), jnp.full((tq, 1), NEG_INF, F32), s_even, p_even)
    n_iter = (n_steps - 2) // unroll
    carry = lax.fori_loop(0, n_iter, body, carry)
    for t in range(n_iter * unroll + 1, n_steps):
        carry = time_step(jnp.int32(t), t % 2, *carry, with_scores=t + 1 < n_steps)
    values(jnp.int32(n_steps - 1), carry[1], p_buf[(n_steps - 1) % 2])
    for pair in range(MLA_PAIRS):
        halves = []
        for h in (2 * pair, 2 * pair + 1):
            acc = acc_scr[h]
            halves.append(acc[:, :LANES] / acc[:, LANES:])
        o_ref[pair] = jnp.where(lane < MLA_V, halves[0], halves[1]).astype(BF16)


def _mla(q, k, v, tq=512, tk=1024, unroll=6):
    bsz, nh, s, w = q.shape
    return pl.pallas_call(
        functools.partial(_mla_kernel, tk=tk, unroll=unroll),
        grid=(bsz, s // tq),
        in_specs=[
            pl.BlockSpec((None, nh, tq, w), lambda b, i: (b, 0, i, 0)),
            pl.BlockSpec((None, nh, s, w), lambda b, i: (b, 0, 0, 0)),
            pl.BlockSpec((None, v.shape[1], s, w), lambda b, i: (b, 0, 0, 0)),
        ],
        out_specs=pl.BlockSpec((None, v.shape[1], tq, w), lambda b, i: (b, 0, i, 0)),
        out_shape=jax.ShapeDtypeStruct(v.shape, BF16),
        scratch_shapes=[pltpu.VMEM((tq, tk), F32), pltpu.VMEM((tq, tk), F32),
                        pltpu.VMEM((tq, tk), BF16), pltpu.VMEM((tq, tk), BF16),
                        pltpu.VMEM((nh, tq, 2 * LANES), F32)],
        compiler_params=pltpu.CompilerParams(
            dimension_semantics=("arbitrary", "arbitrary"), vmem_limit_bytes=VMEM_LIMIT),
        name="mla",
    )(q, k, v)


def _dil_kernel(q_ref, k_ref, v_ref, o_ref, lse_ref, *, n_side, win):
    tq = q_ref.shape[0]
    seq = k_ref.shape[0]
    qs = pl.program_id(2) * tq
    if win < seq:
        ks = pl.multiple_of(jnp.clip(qs - n_side, 0, seq - win), n_side)
    else:
        ks = 0
    q = q_ref[...]
    kw = k_ref[pl.ds(ks, win), :]
    vw = v_ref[pl.ds(ks, win), :]
    qpos = qs + lax.broadcasted_iota(jnp.int32, (tq, 1), 0)
    kpos = ks + lax.broadcasted_iota(jnp.int32, (1, win), 1)
    valid = jnp.abs(kpos - qpos) <= n_side
    lane_head = lax.broadcasted_iota(jnp.int32, (1, DIL_GROUP_W), 1) // DIL_HEAD_DIM
    o = jnp.zeros((tq, DIL_GROUP_W), F32)
    lse = jnp.zeros((tq, DIL_GROUP_W), F32)
    for h in range(DIL_HEADS):
        hm = lane_head == h
        qh = jnp.where(hm, q, jnp.zeros_like(q))
        s = jnp.where(valid, _dot_nt(qh, kw), NEG_INF)
        m = jnp.max(s, axis=-1, keepdims=True)
        p = jnp.exp(s - m)
        l = jnp.sum(p, axis=-1, keepdims=True)
        pv = _dot(p.astype(BF16), vw)
        o = jnp.where(hm, pv / l, o)
        lse = jnp.where(hm, m + jnp.log(l), lse)
    o_ref[...] = o.astype(BF16)
    lse_ref[...] = lse


def _dilated(qkv, n_side, tq=256):
    bsz, dil, seq, _ = qkv.shape
    w = DIL_GROUP_W
    win = min(seq, tq + 2 * n_side)
    qspec = pl.BlockSpec((None, None, tq, w), lambda b, r, i: (b, r, i, 0))
    kspec = pl.BlockSpec((None, None, seq, w), lambda b, r, i: (b, r, 0, 1))
    vspec = pl.BlockSpec((None, None, seq, w), lambda b, r, i: (b, r, 0, 2))
    out = (bsz, dil, seq, w)
    return pl.pallas_call(
        functools.partial(_dil_kernel, n_side=n_side, win=win),
        grid=(bsz, dil, seq // tq),
        in_specs=[qspec, kspec, vspec],
        out_specs=(qspec, qspec),
        out_shape=(jax.ShapeDtypeStruct(out, BF16), jax.ShapeDtypeStruct(out, F32)),
        compiler_params=pltpu.CompilerParams(
            dimension_semantics=("arbitrary",) * 3, vmem_limit_bytes=VMEM_LIMIT),
        name="dilated",
    )(qkv, qkv, qkv)


def _post_kernel(x_ref, shift_ref, scale_ref, gate_ref, gpre_ref, gpost_ref, wgate_ref,
                 omla_ref, o0_ref, o1_ref, o2_ref, l0_ref, l1_ref, l2_ref,
                 womla_ref, wodil_ref, wout_ref, out_ref, merge_scr):
    x = x_ref[...]
    tm = x.shape[0]
    u = (_rms(x, gpre_ref[...]) * (1.0 + scale_ref[...]) + shift_ref[...]).astype(BF16)
    gate_a = jax.nn.sigmoid(_dot(u, wgate_ref[:, :D_MODEL]))
    gate_b = jax.nn.sigmoid(_dot(u, wgate_ref[:, D_MODEL:]))

    def natural(ref, slot):
        dil = ref.shape[0]
        if dil == 1:
            return ref[0].astype(F32)
        halves = DIL_GROUP_W // LANES
        for r in range(dil):
            blk = ref[r].astype(F32)
            for hf in range(halves):
                merge_scr[slot * halves + hf, pl.ds(r, tm // dil, stride=dil), :] = (
                    blk[:, hf * LANES:(hf + 1) * LANES])
        return jnp.concatenate([merge_scr[slot * halves + hf] for hf in range(halves)], axis=-1)

    l0, l1, l2 = natural(l0_ref, 0), natural(l1_ref, 1), natural(l2_ref, 2)
    mx = jnp.maximum(jnp.maximum(l0, l1), l2)
    e0, e1, e2 = jnp.exp(l0 - mx), jnp.exp(l1 - mx), jnp.exp(l2 - mx)
    inv = 1.0 / (e0 + e1 + e2)
    o_dil = ((e0 * inv) * natural(o0_ref, 3) + (e1 * inv) * natural(o1_ref, 4)
             + (e2 * inv) * natural(o2_ref, 5))
    o_mla = jnp.concatenate([omla_ref[p] for p in range(MLA_PAIRS)], axis=-1)
    merged = (gate_a * _dot(o_mla, womla_ref[...])
              + gate_b * _dot(o_dil.astype(BF16), wodil_ref[...]))
    y = _dot(merged.astype(BF16), wout_ref[...])
    out_ref[...] = x + gate_ref[...] * _rms(y, gpost_ref[...])


def _post(x, shift, scale, gate, g_pre, g_post, w_gates, o_mla, o_dil, lse_dil,
          w_o_mla, w_o_dil, w_out, tm=512):
    bsz, s, d = x.shape
    tok = lambda n: pl.BlockSpec((None, tm, n), lambda b, i: (b, i, 0))
    mod = pl.BlockSpec((None, 1, d), lambda b, i: (b, 0, 0))
    gw = DIL_GROUP_W
    split = [pl.BlockSpec((None, dil, tm // dil, gw), lambda b, i: (b, 0, i, 0))
             for _, dil in DIL_PATTERNS]
    return pl.pallas_call(
        _post_kernel,
        grid=(bsz, s // tm),
        in_specs=[tok(d), mod, mod, mod, _const_spec((1, d)), _const_spec((1, d)),
                  _const_spec(w_gates.shape),
                  pl.BlockSpec((None, MLA_PAIRS, tm, LANES), lambda b, i: (b, 0, i, 0)),
                  *split, *split,
                  _const_spec(w_o_mla.shape), _const_spec(w_o_dil.shape),
                  _const_spec(w_out.shape)],
        out_specs=tok(d),
        out_shape=jax.ShapeDtypeStruct(x.shape, F32),
        scratch_shapes=[pltpu.VMEM((2 * N_DIL_GROUPS * gw // LANES, tm, LANES), F32)],
        compiler_params=pltpu.CompilerParams(
            dimension_semantics=("arbitrary", "arbitrary"), vmem_limit_bytes=VMEM_LIMIT),
        name="post",
    )(x, shift, scale, gate, g_pre, g_post, w_gates, o_mla, *o_dil, *lse_dil,
      w_o_mla, w_o_dil, w_out)


def _tables():
    t = np.zeros((N_TABLE_ROWS, LANES), np.float32)
    scale = np.float32((MLA_NOPE + MLA_ROPE) ** -0.5 * LOG2E)
    half = MLA_ROPE // 2
    f_m = MLA_THETA ** (-np.arange(0, MLA_ROPE, 2, dtype=np.float32) / MLA_ROPE)
    t[ROW_INVF_Q, MLA_NOPE:] = np.tile(f_m, 4)
    t[ROW_MC_Q, :MLA_NOPE + MLA_ROPE] = scale
    t[ROW_MS_Q, MLA_NOPE + MLA_ROPE:MLA_NOPE + MLA_ROPE + half] = -scale
    t[ROW_MS_Q, MLA_NOPE + MLA_ROPE + half:] = scale
    t[ROW_INVF_K, :2 * MLA_ROPE] = np.tile(f_m, 4)
    t[ROW_MC_K, :MLA_ROPE] = 1.0
    t[ROW_MS_K, MLA_ROPE:MLA_ROPE + half] = -1.0
    t[ROW_MS_K, MLA_ROPE + half:2 * MLA_ROPE] = 1.0
    f_d = ROPE_THETA ** (-np.arange(0, ROPE_PART, 2, dtype=np.float32) / ROPE_PART)
    hp = ROPE_PART // 2
    for off in (0, DIL_HEAD_DIM):
        t[ROW_INVF_D, off:off + ROPE_PART] = np.tile(f_d, 2)
        t[ROW_M2_D, off:off + hp] = -1.0
        t[ROW_M1_D, off + hp:off + ROPE_PART] = 1.0
    return jnp.asarray(t)


def _layout_weights(w_in, w_uq, w_ukv):
    o_ckv = MLA_Q_LORA
    o_kr = o_ckv + MLA_KV_LORA
    o_qkv = o_kr + MLA_ROPE
    o_ga = o_qkv + DIL_QKV
    half = MLA_ROPE // 2
    k_raw = w_in[:, o_kr:o_qkv]
    k_swap = jnp.concatenate([k_raw[:, half:], k_raw[:, :half]], axis=1)
    pad = jnp.zeros((D_MODEL, 2 * MLA_Q_LORA - MLA_Q_LORA - MLA_KV_LORA - 2 * MLA_ROPE), w_in.dtype)
    w_in2 = jnp.concatenate(
        [w_in[:, :o_kr], k_raw, k_swap, pad, w_in[:, o_qkv:o_ga]], axis=1).astype(BF16)
    w_gates = w_in[:, o_ga:].astype(BF16)

    wq = w_uq.reshape(MLA_Q_LORA, MLA_HEADS, MLA_NOPE + MLA_ROPE)
    nope, t1, t2 = wq[..., :MLA_NOPE], wq[..., MLA_NOPE:MLA_NOPE + half], wq[..., MLA_NOPE + half:]
    w_uq2 = jnp.concatenate([nope, t1, t2, t2, t1], axis=-1).reshape(
        MLA_Q_LORA, MLA_HEADS * MLA_PAD).astype(BF16)

    wkv = w_ukv.reshape(MLA_KV_LORA, MLA_HEADS, MLA_NOPE + MLA_V)
    k_cols = jnp.concatenate(
        [wkv[..., :MLA_NOPE], jnp.zeros((MLA_KV_LORA, MLA_HEADS, MLA_PAD - MLA_NOPE), w_ukv.dtype)],
        axis=-1).reshape(MLA_KV_LORA, MLA_HEADS * MLA_PAD)
    v_cols = wkv[..., MLA_NOPE:].reshape(MLA_KV_LORA, MLA_HEADS * MLA_V)
    place = np.zeros((MLA_ROPE, MLA_HEADS, MLA_PAD), np.float32)
    for i in range(MLA_ROPE):
        place[i, :, MLA_NOPE + i] = 1.0
        place[i, :, MLA_NOPE + MLA_ROPE + i] = 1.0
    place = jnp.asarray(place.reshape(MLA_ROPE, MLA_HEADS * MLA_PAD))
    n_lhs = 2 * MLA_Q_LORA - MLA_Q_LORA
    k_rows = jnp.concatenate(
        [k_cols, place, place,
         jnp.zeros((n_lhs - MLA_KV_LORA - 2 * MLA_ROPE, MLA_HEADS * MLA_PAD), F32)], axis=0)
    v_rows = jnp.concatenate(
        [v_cols, jnp.zeros((n_lhs - MLA_KV_LORA, MLA_HEADS * MLA_V), F32)], axis=0)
    w_kv = jnp.concatenate([k_rows, v_rows], axis=1).astype(BF16)
    return w_in2, w_gates, w_uq2, w_kv


def _ff_weights(w_gate, w_up, w_down):
    return w_gate.astype(BF16), w_up.astype(BF16), w_down.astype(BF16)


def kernel(x, c, positions, w_ada, b_ada, g_pre_ff1, w_gate1, w_up1, w_down1, g_post_ff1, g_pre_mix, w_in, g_cq, w_uq, g_ckv, w_ukv, w_o_mla, w_o_dil, w_out, g_post_mix, g_pre_ff2, w_gate2, w_up2, w_down2, g_post_ff2):
    bsz, s, d = x.shape
    depth = w_ada.shape[0]
    pos = positions.astype(F32).reshape(bsz, s, 1)
    tables = _tables()
    for l in range(depth):
        ada = _ada(c, w_ada[l], b_ada[l]).reshape(bsz, 3, 3, 1, d)
        mod = lambda i, j: ada[:, i, j]

        x = _ffn(x, mod(0, 0), mod(0, 1), mod(0, 2), g_pre_ff1[l][None], g_post_ff1[l][None],
                 *_ff_weights(w_gate1[l], w_up1[l], w_down1[l]))

        w_in2, w_gates, w_uq2, w_kv = _layout_weights(w_in[l], w_uq[l], w_ukv[l])
        q_m, k_m, v_m, *dil_qkv = _proj(x, pos, mod(1, 0), mod(1, 1), g_pre_mix[l][None], tables,
                                        w_in2, g_cq[l][None], w_uq2, g_ckv[l][None], w_kv)
        o_mla = _mla(q_m, k_m, v_m)

        outs, lses = [], []
        for qkv_g, (window, dil) in zip(dil_qkv, DIL_PATTERNS):
            o_g, lse_g = _dilated(qkv_g, n_side=window // (2 * dil))
            outs.append(o_g)
            lses.append(lse_g)

        x = _post(x, mod(1, 0), mod(1, 1), mod(1, 2), g_pre_mix[l][None], g_post_mix[l][None],
                  w_gates, o_mla, outs, lses,
                  w_o_mla[l].astype(BF16), w_o_dil[l].astype(BF16), w_out[l].astype(BF16))

        x = _ffn(x, mod(2, 0), mod(2, 1), mod(2, 2), g_pre_ff2[l][None], g_post_ff2[l][None],
                 *_ff_weights(w_gate2[l], w_up2[l], w_down2[l]))
    return x
```

```python
import functools

import numpy as np
import jax
import jax.numpy as jnp
from jax import lax
from jax.experimental import pallas as pl
from jax.experimental.pallas import tpu as pltpu

F32 = jnp.float32
BF16 = jnp.bfloat16

D_MODEL = 1024
D_FF = 2816
FFN_RES = 0.5
MLA_HEADS = 8
MLA_Q_LORA = 256
MLA_KV_LORA = 128
MLA_NOPE = 64
MLA_ROPE = 32
MLA_V = 64
MLA_THETA = 10000.0
DIL_PATTERNS = ((128, 1), (512, 4), (2048, 16))
N_DIL_GROUPS = 3
DIL_HEADS = 4
DIL_HEAD_DIM = 64
DIL_GROUP_W = DIL_HEADS * DIL_HEAD_DIM
DIL_QKV = 3 * N_DIL_GROUPS * DIL_GROUP_W
ROPE_THETA = 500000.0
ROPE_PART = DIL_HEAD_DIM // 4
NORM_EPS = 1e-6
NEG_INF = -1e30

LANES = 128
MXU_N = 256
FF_CHUNK = MXU_N
N_FF_CHUNKS = D_FF // FF_CHUNK
MLA_PAD = LANES
MLA_PAIRS = MLA_HEADS * MLA_V // LANES
LOG2E = float(np.log2(np.e))
LN2 = float(np.log(2.0))
DIL_STEP_QUERIES = 1024
FFN_ROW_GROUPS = 2
VMEM_LIMIT = 56 * 1024 * 1024

ROW_INVF, ROW_ONE_Q, ROW_MC_Q, ROW_MS_Q, ROW_MC_K, ROW_MS_K, ROW_M1_D, ROW_M2_D = range(8)
N_TABLE_ROWS = 8


def _rms(x, g):
    return x * lax.rsqrt(jnp.mean(x * x, axis=-1, keepdims=True) + NORM_EPS) * g


def _silu(x):
    return x * jax.nn.sigmoid(x)


def _dot(a, b):
    return jnp.dot(a, b, preferred_element_type=F32)


def _dot_nt(a, b):
    return lax.dot_general(a, b, (((1,), (1,)), ((), ())), preferred_element_type=F32)


def _const_spec(shape):
    nd = len(shape)
    return pl.BlockSpec(shape, lambda *_: (0,) * nd, pipeline_mode=pl.Buffered(1))


def _ada_kernel(c_ref, w_ref, b_ref, o_ref):
    cond = _silu(c_ref[...])
    o_ref[...] = _dot(cond, w_ref[...]) + b_ref[...]


def _ada(c, w, b):
    bsz, d = c.shape
    n = w.shape[1]
    tn = 1024
    return pl.pallas_call(
        _ada_kernel,
        grid=(n // tn,),
        in_specs=[
            pl.BlockSpec((bsz, d), lambda j: (0, 0)),
            pl.BlockSpec((d, tn), lambda j: (0, j)),
            pl.BlockSpec((1, tn), lambda j: (0, j)),
        ],
        out_specs=pl.BlockSpec((bsz, tn), lambda j: (0, j)),
        out_shape=jax.ShapeDtypeStruct((bsz, n), F32),
        compiler_params=pltpu.CompilerParams(dimension_semantics=("arbitrary",)),
        name="ada",
    )(c, w, b.reshape(1, n))


def _ffn_kernel(x_ref, shift_ref, scale_ref, gate_ref, gpre_ref, gpost_ref,
                wg_ref, wu_ref, wd_ref, o_ref):
    rows_per = x_ref.shape[0] // FFN_ROW_GROUPS
    parts = [slice(part * rows_per, (part + 1) * rows_per) for part in range(FFN_ROW_GROUPS)]
    hbs = [(_rms(x_ref[rows, :], gpre_ref[...]) * (1.0 + scale_ref[...])
            + shift_ref[...]).astype(BF16) for rows in parts]
    accs = [None] * FFN_ROW_GROUPS

    def gate_up(part, c):
        cols = slice(c * FF_CHUNK, (c + 1) * FF_CHUNK)
        return part, c, _dot(hbs[part], wg_ref[:, cols]), _dot(hbs[part], wu_ref[:, cols])

    def down(part, c, g, u):
        a = (_silu(g) * u).astype(BF16)
        d = _dot(a, wd_ref[c * FF_CHUNK:(c + 1) * FF_CHUNK, :])
        accs[part] = d if accs[part] is None else accs[part] + d
        if c == N_FF_CHUNKS - 1:
            rows = parts[part]
            y = _rms(accs[part], gpost_ref[...])
            o_ref[rows, :] = x_ref[rows, :] + (FFN_RES * gate_ref[...]) * y

    items = [(part, c) for part in range(FFN_ROW_GROUPS) for c in range(N_FF_CHUNKS)]
    pending = gate_up(*items[0])
    for nxt in items[1:]:
        upcoming = gate_up(*nxt)
        down(*pending)
        pending = upcoming
    down(*pending)


def _ffn(x, shift, scale, gate, g_pre, g_post, wg, wu, wd, tm=1024):
    bsz, s, d = x.shape
    tok = pl.BlockSpec((None, tm, d), lambda b, i: (b, i, 0))
    mod = pl.BlockSpec((None, 1, d), lambda b, i: (b, 0, 0))
    return pl.pallas_call(
        _ffn_kernel,
        grid=(bsz, s // tm),
        in_specs=[tok, mod, mod, mod, _const_spec((1, d)), _const_spec((1, d)),
                  _const_spec(wg.shape), _const_spec(wu.shape), _const_spec(wd.shape)],
        out_specs=tok,
        out_shape=jax.ShapeDtypeStruct(x.shape, F32),
        compiler_params=pltpu.CompilerParams(
            dimension_semantics=("arbitrary", "arbitrary"), vmem_limit_bytes=VMEM_LIMIT),
        name="ffn",
    )(x, shift, scale, gate, g_pre, g_post, wg, wu, wd)


def _proj_kernel(x_ref, pos_ref, shift_ref, scale_ref, gpre_ref, tab_ref, win_ref,
                 gcq_ref, wuq_ref, gckv_ref, wkv_ref,
                 q_ref, k_ref, v_ref, d0_ref, d1_ref, d2_ref, split_scr):
    d_refs = (d0_ref, d1_ref, d2_ref)
    x = x_ref[...]
    u = (_rms(x, gpre_ref[...]) * (1.0 + scale_ref[...]) + shift_ref[...]).astype(BF16)
    pos = pos_ref[...]

    def row(r):
        return tab_ref[r:r + 1, :]

    ang = pos * row(ROW_INVF)
    cos_a, sin_a = jnp.cos(ang), jnp.sin(ang)
    cos_r, sin_r = pltpu.roll(cos_a, LANES // 2, 1), pltpu.roll(sin_a, LANES // 2, 1)
    low_half = lax.broadcasted_iota(jnp.int32, (1, LANES), 1) < LANES // 2

    c_q = _dot(u, win_ref[:, 0:MLA_Q_LORA])
    q_pad = _dot(_rms(c_q, gcq_ref[...]).astype(BF16), wuq_ref[...])
    t_q = row(ROW_ONE_Q) + cos_a * row(ROW_MC_Q) + sin_a * row(ROW_MS_Q)
    for h in range(MLA_HEADS):
        q_ref[h] = (q_pad[:, h * MLA_PAD:(h + 1) * MLA_PAD] * t_q).astype(BF16)

    pb = _dot(u, win_ref[:, MLA_Q_LORA:2 * MLA_Q_LORA])
    c_kv = pb[:, :MLA_KV_LORA]
    r_kv = lax.rsqrt(jnp.mean(c_kv * c_kv, axis=-1, keepdims=True) + NORM_EPS)
    t_k = cos_r * row(ROW_MC_K) + sin_r * row(ROW_MS_K)
    lhs = jnp.concatenate([c_kv * r_kv * gckv_ref[...], pb[:, MLA_KV_LORA:] * t_k], axis=-1)
    kv = _dot(lhs.astype(BF16), wkv_ref[...])
    for h in range(MLA_HEADS):
        k_ref[h] = kv[:, h * MLA_PAD:(h + 1) * MLA_PAD].astype(BF16)
    v_off = MLA_HEADS * MLA_PAD
    for p in range(MLA_PAIRS):
        v_ref[p] = kv[:, v_off + p * LANES: v_off + (p + 1) * LANES].astype(BF16)

    cd = jnp.where(low_half, cos_a, cos_r)
    sd = jnp.where(low_half, sin_a, sin_r)
    s1 = sd * row(ROW_M1_D)
    s2 = sd * row(ROW_M2_D)
    q_scale = DIL_HEAD_DIM ** -0.5 * LOG2E
    base = 2 * MLA_Q_LORA
    n_chunks = DIL_QKV // MXU_N
    rope_chunks = 2 * N_DIL_GROUPS * DIL_GROUP_W // MXU_N
    q_chunks = N_DIL_GROUPS * DIL_GROUP_W // MXU_N
    tm = x.shape[0]
    for j in range(n_chunks):
        which, g = divmod(j, N_DIL_GROUPS)
        dil = DIL_PATTERNS[g][1]
        d_ref = d_refs[g]
        cols = slice(which * DIL_GROUP_W, (which + 1) * DIL_GROUP_W)
        pc = _dot(u, win_ref[:, base + j * MXU_N: base + (j + 1) * MXU_N])
        for half in range(MXU_N // LANES):
            xs = pc[:, half * LANES:(half + 1) * LANES]
            if j < rope_chunks:
                xs = (xs * cd + pltpu.roll(xs, ROPE_PART // 2, 1) * s1
                      + pltpu.roll(xs, LANES - ROPE_PART // 2, 1) * s2)
                if j < q_chunks:
                    xs = xs * q_scale
            lo = cols.start + half * LANES
            if dil == 1:
                d_ref[0, :, lo:lo + LANES] = xs.astype(BF16)
            else:
                split_scr[half] = xs
                for r in range(dil):
                    d_ref[r, :, lo:lo + LANES] = (
                        split_scr[half, pl.ds(r, tm // dil, stride=dil), :].astype(BF16))


def _proj(x, pos, shift, scale, g_pre, tables, w_in2, g_cq, w_uq2, g_ckv, w_kv, tm=512):
    bsz, s, d = x.shape
    tok = lambda n: pl.BlockSpec((None, tm, n), lambda b, i: (b, i, 0))
    mod = pl.BlockSpec((None, 1, d), lambda b, i: (b, 0, 0))
    heads = lambda n: pl.BlockSpec((None, n, tm, LANES), lambda b, i: (b, 0, i, 0))
    out_shapes = (
        jax.ShapeDtypeStruct((bsz, MLA_HEADS, s, MLA_PAD), BF16),
        jax.ShapeDtypeStruct((bsz, MLA_HEADS, s, MLA_PAD), BF16),
        jax.ShapeDtypeStruct((bsz, MLA_PAIRS, s, LANES), BF16),
    ) + tuple(jax.ShapeDtypeStruct((bsz, dil, s // dil, 3 * DIL_GROUP_W), BF16)
              for _, dil in DIL_PATTERNS)
    split = lambda dil: pl.BlockSpec((None, dil, tm // dil, 3 * DIL_GROUP_W),
                                     lambda b, i: (b, 0, i, 0))
    return pl.pallas_call(
        _proj_kernel,
        grid=(bsz, s // tm),
        in_specs=[tok(d), tok(1), mod, mod, _const_spec((1, d)), _const_spec(tables.shape),
                  _const_spec(w_in2.shape), _const_spec(g_cq.shape), _const_spec(w_uq2.shape),
                  _const_spec(g_ckv.shape), _const_spec(w_kv.shape)],
        out_specs=(heads(MLA_HEADS), heads(MLA_HEADS), heads(MLA_PAIRS))
        + tuple(split(dil) for _, dil in DIL_PATTERNS),
        out_shape=out_shapes,
        scratch_shapes=[pltpu.VMEM((MXU_N // LANES, tm, LANES), F32)],
        compiler_params=pltpu.CompilerParams(
            dimension_semantics=("arbitrary", "arbitrary"), vmem_limit_bytes=VMEM_LIMIT),
        name="proj",
    )(x, pos, shift, scale, g_pre, tables, w_in2, g_cq, w_uq2, g_ckv, w_kv)


def _mla_kernel(q_ref, k_ref, v_ref, o_ref, s_even, s_odd, p_even, p_odd, acc_scr, *,
                tk, unroll):
    tq = q_ref.shape[1]
    n_kv = k_ref.shape[1] // tk
    kv_bits = n_kv.bit_length() - 1
    assert n_kv == 1 << kv_bits and n_kv % 2 == 0 and unroll % 2 == 0
    n_steps = MLA_HEADS * n_kv
    lane = lax.broadcasted_iota(jnp.int32, (1, LANES), 1)
    ones = jnp.ones((tk, LANES), BF16)

    def split(t):
        return lax.shift_right_logical(t, kv_bits), lax.bitwise_and(t, n_kv - 1)

    def scores(t, s_scr):
        h, j = split(t)
        rows = pl.ds(pl.multiple_of(j * tk, tk), tk)
        s_scr[...] = _dot_nt(q_ref[h], k_ref[h, rows, :])

    def softmax(t, m, s_scr, p_scr):
        _, j = split(t)
        s = s_scr[...]
        m_prev = jnp.where(j == 0, NEG_INF, m)
        m_new = jnp.maximum(m_prev, jnp.max(s, axis=-1, keepdims=True))
        p_scr[...] = jnp.exp2(s - m_new).astype(BF16)
        return m_new, jnp.exp2(m_prev - m_new)

    def values(t, alpha, p_scr):
        h, j = split(t)
        rows = pl.ds(pl.multiple_of(j * tk, tk), tk)
        rhs = jnp.concatenate([v_ref[lax.shift_right_logical(h, 1), rows, :], ones], axis=1)
        acc_scr[h] = alpha * acc_scr[h] + _dot(p_scr[...], rhs)

    s_buf, p_buf = (s_even, s_odd), (p_even, p_odd)

    def time_step(t, par, m, alpha_prev, with_scores=True):
        if with_scores:
            scores(t + 1, s_buf[1 - par])
        m, alpha = softmax(t, m, s_buf[par], p_buf[par])
        values(t - 1, alpha_prev, p_buf[1 - par])
        return m, alpha

    def body(u, carry):
        t0 = unroll * u + 1
        for i in range(unroll):
            carry = time_step(t0 + i, (1 + i) % 2, *carry)
        return carry

    acc_scr[...] = jnp.zeros(acc_scr.shape, F32)
    scores(jnp.int32(0), s_even)
    scores(jnp.int32(1), s_odd)
    carry = softmax(jnp.int32(0), jnp.full((tq, 1), NEG_INF, F32), s_even, p_even)
    n_iter = (n_steps - 2) // unroll
    carry = lax.fori_loop(0, n_iter, body, carry)
    for t in range(n_iter * unroll + 1, n_steps):
        carry = time_step(jnp.int32(t), t % 2, *carry, with_scores=t + 1 < n_steps)
    values(jnp.int32(n_steps - 1), carry[1], p_buf[(n_steps - 1) % 2])
    for pair in range(MLA_PAIRS):
        halves = []
        for h in (2 * pair, 2 * pair + 1):
            acc = acc_scr[h]
            halves.append(acc[:, :LANES] / acc[:, LANES:])
        o_ref[pair] = jnp.where(lane < MLA_V, halves[0], halves[1]).astype(BF16)


def _mla(q, k, v, tq=512, tk=1024, unroll=6):
    bsz, nh, s, w = q.shape
    return pl.pallas_call(
        functools.partial(_mla_kernel, tk=tk, unroll=unroll),
        grid=(bsz, s // tq),
        in_specs=[
            pl.BlockSpec((None, nh, tq, w), lambda b, i: (b, 0, i, 0)),
            pl.BlockSpec((None, nh, s, w), lambda b, i: (b, 0, 0, 0)),
            pl.BlockSpec((None, v.shape[1], s, w), lambda b, i: (b, 0, 0, 0)),
        ],
        out_specs=pl.BlockSpec((None, v.shape[1], tq, w), lambda b, i: (b, 0, i, 0)),
        out_shape=jax.ShapeDtypeStruct(v.shape, BF16),
        scratch_shapes=[pltpu.VMEM((tq, tk), F32), pltpu.VMEM((tq, tk), F32),
                        pltpu.VMEM((tq, tk), BF16), pltpu.VMEM((tq, tk), BF16),
                        pltpu.VMEM((nh, tq, 2 * LANES), F32)],
        compiler_params=pltpu.CompilerParams(
            dimension_semantics=("arbitrary", "arbitrary"), vmem_limit_bytes=VMEM_LIMIT),
        name="mla",
    )(q, k, v)


def _dil_kernel(q_ref, k_ref, v_ref, o_ref, lse_ref, *, n_side, sub):
    n_res, chunk, _ = q_ref.shape
    seq = k_ref.shape[1]
    win = sub + 2 * n_side
    base = 0 if chunk == seq else pl.program_id(2) * chunk
    rel = (lax.broadcasted_iota(jnp.int32, (sub, win), 1)
           - lax.broadcasted_iota(jnp.int32, (sub, win), 0))
    lane_head = lax.broadcasted_iota(jnp.int32, (1, DIL_GROUP_W), 1) // DIL_HEAD_DIM
    head_masks = [lane_head == h for h in range(DIL_HEADS)]

    def scores(r, t):
        qs = base + t * sub
        if isinstance(qs, int):
            ks = min(max(qs - n_side, 0), seq - win)
        else:
            ks = pl.multiple_of(jnp.clip(qs - n_side, 0, seq - win), n_side)
        q = q_ref[r, t * sub:(t + 1) * sub, :]
        kw = k_ref[r, pl.ds(ks, win), :]
        valid = jnp.abs(rel + (ks - qs)) <= n_side
        s = [jnp.where(valid, _dot_nt(jnp.where(hm, q, jnp.zeros_like(q)), kw), NEG_INF)
             for hm in head_masks]
        return r, t, ks, s

    def finish(r, t, ks, s):
        vw = v_ref[r, pl.ds(ks, win), :]
        m = [jnp.max(sh, axis=-1, keepdims=True) for sh in s]
        p = [jnp.exp2(sh - mh) for sh, mh in zip(s, m)]
        l = [jnp.sum(ph, axis=-1, keepdims=True) for ph in p]
        pv = [_dot(ph.astype(BF16), vw) for ph in p]
        o = jnp.zeros((sub, DIL_GROUP_W), F32)
        lse = jnp.zeros((sub, DIL_GROUP_W), F32)
        for hm, mh, lh, pvh in zip(head_masks, m, l, pv):
            o = jnp.where(hm, pvh / lh, o)
            lse = jnp.where(hm, mh * LN2 + jnp.log(lh), lse)
        o_ref[r, t * sub:(t + 1) * sub, :] = o.astype(BF16)
        lse_ref[r, t * sub:(t + 1) * sub, :] = lse

    tiles = [(r, t) for r in range(n_res) for t in range(chunk // sub)]
    pending = scores(*tiles[0])
    for nxt in tiles[1:]:
        upcoming = scores(*nxt)
        finish(*pending)
        pending = upcoming
    finish(*pending)


def _dilated(qkv, n_side, n_res, chunk, sub=128):
    bsz, dil, seq, _ = qkv.shape
    w = DIL_GROUP_W
    qspec = pl.BlockSpec((None, n_res, chunk, w), lambda b, r, i: (b, r, i, 0))
    kspec = pl.BlockSpec((None, n_res, seq, w), lambda b, r, i: (b, r, 0, 1))
    vspec = pl.BlockSpec((None, n_res, seq, w), lambda b, r, i: (b, r, 0, 2))
    out = (bsz, dil, seq, w)
    return pl.pallas_call(
        functools.partial(_dil_kernel, n_side=n_side, sub=sub),
        grid=(bsz, dil // n_res, seq // chunk),
        in_specs=[qspec, kspec, vspec],
        out_specs=(qspec, qspec),
        out_shape=(jax.ShapeDtypeStruct(out, BF16), jax.ShapeDtypeStruct(out, F32)),
        compiler_params=pltpu.CompilerParams(
            dimension_semantics=("arbitrary",) * 3, vmem_limit_bytes=VMEM_LIMIT),
        name="dilated",
    )(qkv, qkv, qkv)


def _post_kernel(x_ref, shift_ref, scale_ref, gate_ref, gpre_ref, gpost_ref, wgate_ref,
                 omla_ref, o0_ref, o1_ref, o2_ref, l0_ref, l1_ref, l2_ref,
                 womla_ref, wodil_ref, wout_ref, out_ref, merge_scr):
    x = x_ref[...]
    tm = x.shape[0]
    u = (_rms(x, gpre_ref[...]) * (1.0 + scale_ref[...]) + shift_ref[...]).astype(BF16)
    gate_a = jax.nn.sigmoid(_dot(u, wgate_ref[:, :D_MODEL]))
    gate_b = jax.nn.sigmoid(_dot(u, wgate_ref[:, D_MODEL:]))

    def natural(ref, slot):
        dil = ref.shape[0]
        if dil == 1:
            return ref[0].astype(F32)
        halves = DIL_GROUP_W // LANES
        for r in range(dil):
            blk = ref[r].astype(F32)
            for hf in range(halves):
                merge_scr[slot * halves + hf, pl.ds(r, tm // dil, stride=dil), :] = (
                    blk[:, hf * LANES:(hf + 1) * LANES])
        return jnp.concatenate([merge_scr[slot * halves + hf] for hf in range(halves)], axis=-1)

    l0, l1, l2 = natural(l0_ref, 0), natural(l1_ref, 1), natural(l2_ref, 2)
    mx = jnp.maximum(jnp.maximum(l0, l1), l2)
    e0, e1, e2 = jnp.exp(l0 - mx), jnp.exp(l1 - mx), jnp.exp(l2 - mx)
    inv = 1.0 / (e0 + e1 + e2)
    o_dil = ((e0 * inv) * natural(o0_ref, 3) + (e1 * inv) * natural(o1_ref, 4)
             + (e2 * inv) * natural(o2_ref, 5))
    o_mla = jnp.concatenate([omla_ref[p] for p in range(MLA_PAIRS)], axis=-1)
    merged = (gate_a * _dot(o_mla, womla_ref[...])
              + gate_b * _dot(o_dil.astype(BF16), wodil_ref[...]))
    y = _dot(merged.astype(BF16), wout_ref[...])
    out_ref[...] = x + gate_ref[...] * _rms(y, gpost_ref[...])


def _post(x, shift, scale, gate, g_pre, g_post, w_gates, o_mla, o_dil, lse_dil,
          w_o_mla, w_o_dil, w_out, tm=512):
    bsz, s, d = x.shape
    tok = lambda n: pl.BlockSpec((None, tm, n), lambda b, i: (b, i, 0))
    mod = pl.BlockSpec((None, 1, d), lambda b, i: (b, 0, 0))
    gw = DIL_GROUP_W
    split = [pl.BlockSpec((None, dil, tm // dil, gw), lambda b, i: (b, 0, i, 0))
             for _, dil in DIL_PATTERNS]
    return pl.pallas_call(
        _post_kernel,
        grid=(bsz, s // tm),
        in_specs=[tok(d), mod, mod, mod, _const_spec((1, d)), _const_spec((1, d)),
                  _const_spec(w_gates.shape),
                  pl.BlockSpec((None, MLA_PAIRS, tm, LANES), lambda b, i: (b, 0, i, 0)),
                  *split, *split,
                  _const_spec(w_o_mla.shape), _const_spec(w_o_dil.shape),
                  _const_spec(w_out.shape)],
        out_specs=tok(d),
        out_shape=jax.ShapeDtypeStruct(x.shape, F32),
        scratch_shapes=[pltpu.VMEM((2 * N_DIL_GROUPS * gw // LANES, tm, LANES), F32)],
        compiler_params=pltpu.CompilerParams(
            dimension_semantics=("arbitrary", "arbitrary"), vmem_limit_bytes=VMEM_LIMIT),
        name="post",
    )(x, shift, scale, gate, g_pre, g_post, w_gates, o_mla, *o_dil, *lse_dil,
      w_o_mla, w_o_dil, w_out)


def _tables():
    t = np.zeros((N_TABLE_ROWS, LANES), np.float32)
    scale = np.float32((MLA_NOPE + MLA_ROPE) ** -0.5 * LOG2E)
    half = MLA_ROPE // 2
    f_m = MLA_THETA ** (-np.arange(0, MLA_ROPE, 2, dtype=np.float32) / MLA_ROPE)
    f_d = ROPE_THETA ** (-np.arange(0, ROPE_PART, 2, dtype=np.float32) / ROPE_PART)
    t[ROW_INVF, :ROPE_PART] = np.tile(f_d, 2)
    t[ROW_INVF, LANES // 2:] = np.tile(f_m, 4)
    t[ROW_ONE_Q, :MLA_NOPE] = scale
    t[ROW_MC_Q, MLA_NOPE:MLA_NOPE + MLA_ROPE] = scale
    t[ROW_MS_Q, MLA_NOPE + MLA_ROPE:MLA_NOPE + MLA_ROPE + half] = -scale
    t[ROW_MS_Q, MLA_NOPE + MLA_ROPE + half:] = scale
    t[ROW_MC_K, :MLA_ROPE] = 1.0
    t[ROW_MS_K, MLA_ROPE:MLA_ROPE + half] = -1.0
    t[ROW_MS_K, MLA_ROPE + half:2 * MLA_ROPE] = 1.0
    hp = ROPE_PART // 2
    for off in (0, DIL_HEAD_DIM):
        t[ROW_M2_D, off:off + hp] = -1.0
        t[ROW_M1_D, off + hp:off + ROPE_PART] = 1.0
    return jnp.asarray(t)


def _layout_weights(w_in, w_uq, w_ukv):
    o_ckv = MLA_Q_LORA
    o_kr = o_ckv + MLA_KV_LORA
    o_qkv = o_kr + MLA_ROPE
    o_ga = o_qkv + DIL_QKV
    half = MLA_ROPE // 2
    k_raw = w_in[:, o_kr:o_qkv]
    k_swap = jnp.concatenate([k_raw[:, half:], k_raw[:, :half]], axis=1)
    pad = jnp.zeros((D_MODEL, 2 * MLA_Q_LORA - MLA_Q_LORA - MLA_KV_LORA - 2 * MLA_ROPE), w_in.dtype)
    w_in2 = jnp.concatenate(
        [w_in[:, :o_kr], k_raw, k_swap, pad, w_in[:, o_qkv:o_ga]], axis=1).astype(BF16)
    w_gates = w_in[:, o_ga:].astype(BF16)

    wq = w_uq.reshape(MLA_Q_LORA, MLA_HEADS, MLA_NOPE + MLA_ROPE)
    nope, t1, t2 = wq[..., :MLA_NOPE], wq[..., MLA_NOPE:MLA_NOPE + half], wq[..., MLA_NOPE + half:]
    w_uq2 = jnp.concatenate([nope, t1, t2, t2, t1], axis=-1).reshape(
        MLA_Q_LORA, MLA_HEADS * MLA_PAD).astype(BF16)

    wkv = w_ukv.reshape(MLA_KV_LORA, MLA_HEADS, MLA_NOPE + MLA_V)
    k_cols = jnp.concatenate(
        [wkv[..., :MLA_NOPE], jnp.zeros((MLA_KV_LORA, MLA_HEADS, MLA_PAD - MLA_NOPE), w_ukv.dtype)],
        axis=-1).reshape(MLA_KV_LORA, MLA_HEADS * MLA_PAD)
    v_cols = wkv[..., MLA_NOPE:].reshape(MLA_KV_LORA, MLA_HEADS * MLA_V)
    place = np.zeros((MLA_ROPE, MLA_HEADS, MLA_PAD), np.float32)
    for i in range(MLA_ROPE):
        place[i, :, MLA_NOPE + i] = 1.0
        place[i, :, MLA_NOPE + MLA_ROPE + i] = 1.0
    place = jnp.asarray(place.reshape(MLA_ROPE, MLA_HEADS * MLA_PAD))
    n_lhs = 2 * MLA_Q_LORA - MLA_Q_LORA
    k_rows = jnp.concatenate(
        [k_cols, place, place,
         jnp.zeros((n_lhs - MLA_KV_LORA - 2 * MLA_ROPE, MLA_HEADS * MLA_PAD), F32)], axis=0)
    v_rows = jnp.concatenate(
        [v_cols, jnp.zeros((n_lhs - MLA_KV_LORA, MLA_HEADS * MLA_V), F32)], axis=0)
    w_kv = jnp.concatenate([k_rows, v_rows], axis=1).astype(BF16)
    return w_in2, w_gates, w_uq2, w_kv


def _ff_weights(w_gate, w_up, w_down):
    return w_gate.astype(BF16), w_up.astype(BF16), w_down.astype(BF16)


def kernel(x, c, positions, w_ada, b_ada, g_pre_ff1, w_gate1, w_up1, w_down1, g_post_ff1, g_pre_mix, w_in, g_cq, w_uq, g_ckv, w_ukv, w_o_mla, w_o_dil, w_out, g_post_mix, g_pre_ff2, w_gate2, w_up2, w_down2, g_post_ff2):
    bsz, s, d = x.shape
    depth = w_ada.shape[0]
    pos = positions.astype(F32).reshape(bsz, s, 1)
    tables = _tables()
    for l in range(depth):
        ada = _ada(c, w_ada[l], b_ada[l]).reshape(bsz, 3, 3, 1, d)
        mod = lambda i, j: ada[:, i, j]

        x = _ffn(x, mod(0, 0), mod(0, 1), mod(0, 2), g_pre_ff1[l][None], g_post_ff1[l][None],
                 *_ff_weights(w_gate1[l], w_up1[l], w_down1[l]))

        w_in2, w_gates, w_uq2, w_kv = _layout_weights(w_in[l], w_uq[l], w_ukv[l])
        q_m, k_m, v_m, *dil_qkv = _proj(x, pos, mod(1, 0), mod(1, 1), g_pre_mix[l][None], tables,
                                        w_in2, g_cq[l][None], w_uq2, g_ckv[l][None], w_kv)
        o_mla = _mla(q_m, k_m, v_m)

        outs, lses = [], []
        for qkv_g, (window, dil) in zip(dil_qkv, DIL_PATTERNS):
            chunk = min(s // dil, DIL_STEP_QUERIES)
            o_g, lse_g = _dilated(qkv_g, n_side=window // (2 * dil),
                                  n_res=min(dil, DIL_STEP_QUERIES // chunk), chunk=chunk)
            outs.append(o_g)
            lses.append(lse_g)

        x = _post(x, mod(1, 0), mod(1, 1), mod(1, 2), g_pre_mix[l][None], g_post_mix[l][None],
                  w_gates, o_mla, outs, lses,
                  w_o_mla[l].astype(BF16), w_o_dil[l].astype(BF16), w_out[l].astype(BF16))

        x = _ffn(x, mod(2, 0), mod(2, 1), mod(2, 2), g_pre_ff2[l][None], g_post_ff2[l][None],
                 *_ff_weights(w_gate2[l], w_up2[l], w_down2[l]))
    return x
```

```python
import functools

import numpy as np
import jax
import jax.numpy as jnp
from jax import lax
from jax.experimental import pallas as pl
from jax.experimental.pallas import tpu as pltpu

F32 = jnp.float32
BF16 = jnp.bfloat16

D_MODEL = 1024
D_FF = 2816
FFN_RES = 0.5
MLA_HEADS = 8
MLA_Q_LORA = 256
MLA_KV_LORA = 128
MLA_NOPE = 64
MLA_ROPE = 32
MLA_V = 64
MLA_THETA = 10000.0
DIL_PATTERNS = ((128, 1), (512, 4), (2048, 16))
N_DIL_GROUPS = 3
DIL_HEADS = 4
DIL_HEAD_DIM = 64
DIL_GROUP_W = DIL_HEADS * DIL_HEAD_DIM
DIL_QKV = 3 * N_DIL_GROUPS * DIL_GROUP_W
ROPE_THETA = 500000.0
ROPE_PART = DIL_HEAD_DIM // 4
NORM_EPS = 1e-6
NEG_INF = -1e30

LANES = 128
MXU_N = 256
FF_CHUNK = MXU_N
N_FF_CHUNKS = D_FF // FF_CHUNK
MLA_PAD = LANES
MLA_PAIRS = MLA_HEADS * MLA_V // LANES
LOG2E = float(np.log2(np.e))
LN2 = float(np.log(2.0))
DIL_STEP_QUERIES = 1024
FFN_ROW_GROUPS = 2
POST_ROW_GROUPS = 2
VMEM_LIMIT = 56 * 1024 * 1024

ROW_INVF, ROW_ONE_Q, ROW_MC_Q, ROW_MS_Q, ROW_MC_K, ROW_MS_K, ROW_M1_D, ROW_M2_D = range(8)
N_TABLE_ROWS = 8


def _rms(x, g):
    return x * lax.rsqrt(jnp.mean(x * x, axis=-1, keepdims=True) + NORM_EPS) * g


def _silu(x):
    return x * jax.nn.sigmoid(x)


def _dot(a, b):
    return jnp.dot(a, b, preferred_element_type=F32)


def _dot_nt(a, b):
    return lax.dot_general(a, b, (((1,), (1,)), ((), ())), preferred_element_type=F32)


def _const_spec(shape):
    nd = len(shape)
    return pl.BlockSpec(shape, lambda *_: (0,) * nd, pipeline_mode=pl.Buffered(1))


def _ada_kernel(c_ref, w_ref, b_ref, o_ref):
    cond = _silu(c_ref[...])
    o_ref[...] = _dot(cond, w_ref[...]) + b_ref[...]


def _ada(c, w, b):
    bsz, d = c.shape
    n = w.shape[1]
    tn = 1024
    return pl.pallas_call(
        _ada_kernel,
        grid=(n // tn,),
        in_specs=[
            pl.BlockSpec((bsz, d), lambda j: (0, 0)),
            pl.BlockSpec((d, tn), lambda j: (0, j)),
            pl.BlockSpec((1, tn), lambda j: (0, j)),
        ],
        out_specs=pl.BlockSpec((bsz, tn), lambda j: (0, j)),
        out_shape=jax.ShapeDtypeStruct((bsz, n), F32),
        compiler_params=pltpu.CompilerParams(dimension_semantics=("arbitrary",)),
        name="ada",
    )(c, w, b.reshape(1, n))


def _ffn_kernel(x_ref, shift_ref, scale_ref, gate_ref, gpre_ref, gpost_ref,
                wg_ref, wu_ref, wd_ref, o_ref):
    rows_per = x_ref.shape[0] // FFN_ROW_GROUPS
    parts = [slice(part * rows_per, (part + 1) * rows_per) for part in range(FFN_ROW_GROUPS)]
    hbs = [(_rms(x_ref[rows, :], gpre_ref[...]) * (1.0 + scale_ref[...])
            + shift_ref[...]).astype(BF16) for rows in parts]
    accs = [None] * FFN_ROW_GROUPS

    def gate_up(part, c):
        cols = slice(c * FF_CHUNK, (c + 1) * FF_CHUNK)
        return part, c, _dot(hbs[part], wg_ref[:, cols]), _dot(hbs[part], wu_ref[:, cols])

    def down(part, c, g, u):
        a = (_silu(g) * u).astype(BF16)
        d = _dot(a, wd_ref[c * FF_CHUNK:(c + 1) * FF_CHUNK, :])
        accs[part] = d if accs[part] is None else accs[part] + d
        if c == N_FF_CHUNKS - 1:
            rows = parts[part]
            y = _rms(accs[part], gpost_ref[...])
            o_ref[rows, :] = x_ref[rows, :] + (FFN_RES * gate_ref[...]) * y

    items = [(part, c) for part in range(FFN_ROW_GROUPS) for c in range(N_FF_CHUNKS)]
    pending = gate_up(*items[0])
    for nxt in items[1:]:
        upcoming = gate_up(*nxt)
        down(*pending)
        pending = upcoming
    down(*pending)


def _ffn(x, shift, scale, gate, g_pre, g_post, wg, wu, wd, tm=1024):
    bsz, s, d = x.shape
    tok = pl.BlockSpec((None, tm, d), lambda b, i: (b, i, 0))
    mod = pl.BlockSpec((None, 1, d), lambda b, i: (b, 0, 0))
    return pl.pallas_call(
        _ffn_kernel,
        grid=(bsz, s // tm),
        in_specs=[tok, mod, mod, mod, _const_spec((1, d)), _const_spec((1, d)),
                  _const_spec(wg.shape), _const_spec(wu.shape), _const_spec(wd.shape)],
        out_specs=tok,
        out_shape=jax.ShapeDtypeStruct(x.shape, F32),
        compiler_params=pltpu.CompilerParams(
            dimension_semantics=("arbitrary", "arbitrary"), vmem_limit_bytes=VMEM_LIMIT),
        name="ffn",
    )(x, shift, scale, gate, g_pre, g_post, wg, wu, wd)


def _proj_kernel(x_ref, pos_ref, shift_ref, scale_ref, gpre_ref, tab_ref, win_ref,
                 gcq_ref, wuq_ref, gckv_ref, wkv_ref,
                 q_ref, k_ref, v_ref, d0_ref, d1_ref, d2_ref, split_scr):
    d_refs = (d0_ref, d1_ref, d2_ref)
    x = x_ref[...]
    u = (_rms(x, gpre_ref[...]) * (1.0 + scale_ref[...]) + shift_ref[...]).astype(BF16)
    pos = pos_ref[...]

    def row(r):
        return tab_ref[r:r + 1, :]

    ang = pos * row(ROW_INVF)
    cos_a, sin_a = jnp.cos(ang), jnp.sin(ang)
    cos_r, sin_r = pltpu.roll(cos_a, LANES // 2, 1), pltpu.roll(sin_a, LANES // 2, 1)
    low_half = lax.broadcasted_iota(jnp.int32, (1, LANES), 1) < LANES // 2

    held = {}

    def post_cq(c_q):
        held["cq"] = _rms(c_q, gcq_ref[...]).astype(BF16)

    def post_q(q_pad):
        t_q = row(ROW_ONE_Q) + cos_a * row(ROW_MC_Q) + sin_a * row(ROW_MS_Q)
        for h in range(MLA_HEADS):
            q_ref[h] = (q_pad[:, h * MLA_PAD:(h + 1) * MLA_PAD] * t_q).astype(BF16)

    def post_ckv(pb):
        c_kv = pb[:, :MLA_KV_LORA]
        r_kv = lax.rsqrt(jnp.mean(c_kv * c_kv, axis=-1, keepdims=True) + NORM_EPS)
        t_k = cos_r * row(ROW_MC_K) + sin_r * row(ROW_MS_K)
        held["kv_lhs"] = jnp.concatenate(
            [c_kv * r_kv * gckv_ref[...], pb[:, MLA_KV_LORA:] * t_k], axis=-1).astype(BF16)

    def post_kv(kv):
        for h in range(MLA_HEADS):
            k_ref[h] = kv[:, h * MLA_PAD:(h + 1) * MLA_PAD].astype(BF16)
        v_off = MLA_HEADS * MLA_PAD
        for p in range(MLA_PAIRS):
            v_ref[p] = kv[:, v_off + p * LANES: v_off + (p + 1) * LANES].astype(BF16)

    cd = jnp.where(low_half, cos_a, cos_r)
    sd = jnp.where(low_half, sin_a, sin_r)
    s1 = sd * row(ROW_M1_D)
    s2 = sd * row(ROW_M2_D)
    q_scale = DIL_HEAD_DIM ** -0.5 * LOG2E
    base = 2 * MLA_Q_LORA
    n_chunks = DIL_QKV // MXU_N
    rope_chunks = 2 * N_DIL_GROUPS * DIL_GROUP_W // MXU_N
    q_chunks = N_DIL_GROUPS * DIL_GROUP_W // MXU_N
    halves = MXU_N // LANES
    tm = x.shape[0]

    def post_dil(j, pc):
        which, g = divmod(j, N_DIL_GROUPS)
        dil = DIL_PATTERNS[g][1]
        d_ref = d_refs[g]
        for half in range(halves):
            xs = pc[:, half * LANES:(half + 1) * LANES]
            if j < rope_chunks:
                xs = (xs * cd + pltpu.roll(xs, ROPE_PART // 2, 1) * s1
                      + pltpu.roll(xs, LANES - ROPE_PART // 2, 1) * s2)
                if j < q_chunks:
                    xs = xs * q_scale
            lo = which * DIL_GROUP_W + half * LANES
            if dil == 1:
                d_ref[0, :, lo:lo + LANES] = xs.astype(BF16)
            else:
                buf = (j % 2) * halves + half
                split_scr[buf] = xs
                for r in range(dil):
                    d_ref[r, :, lo:lo + LANES] = (
                        split_scr[buf, pl.ds(r, tm // dil, stride=dil), :].astype(BF16))

    def w_in_cols(lo):
        return lambda: _dot(u, win_ref[:, lo:lo + MXU_N])

    stages = [
        (w_in_cols(0), post_cq),
        (w_in_cols(MLA_Q_LORA), post_ckv),
        (w_in_cols(base), functools.partial(post_dil, 0)),
        (lambda: _dot(held["cq"], wuq_ref[...]), post_q),
        (lambda: _dot(held["kv_lhs"], wkv_ref[...]), post_kv),
    ] + [(w_in_cols(base + j * MXU_N), functools.partial(post_dil, j)) for j in range(1, n_chunks)]
    result = stages[0][0]()
    for i, (_, post) in enumerate(stages):
        upcoming = stages[i + 1][0]() if i + 1 < len(stages) else None
        post(result)
        result = upcoming


def _proj(x, pos, shift, scale, g_pre, tables, w_in2, g_cq, w_uq2, g_ckv, w_kv, tm=512):
    bsz, s, d = x.shape
    tok = lambda n: pl.BlockSpec((None, tm, n), lambda b, i: (b, i, 0))
    mod = pl.BlockSpec((None, 1, d), lambda b, i: (b, 0, 0))
    heads = lambda n: pl.BlockSpec((None, n, tm, LANES), lambda b, i: (b, 0, i, 0))
    out_shapes = (
        jax.ShapeDtypeStruct((bsz, MLA_HEADS, s, MLA_PAD), BF16),
        jax.ShapeDtypeStruct((bsz, MLA_HEADS, s, MLA_PAD), BF16),
        jax.ShapeDtypeStruct((bsz, MLA_PAIRS, s, LANES), BF16),
    ) + tuple(jax.ShapeDtypeStruct((bsz, dil, s // dil, 3 * DIL_GROUP_W), BF16)
              for _, dil in DIL_PATTERNS)
    split = lambda dil: pl.BlockSpec((None, dil, tm // dil, 3 * DIL_GROUP_W),
                                     lambda b, i: (b, 0, i, 0))
    return pl.pallas_call(
        _proj_kernel,
        grid=(bsz, s // tm),
        in_specs=[tok(d), tok(1), mod, mod, _const_spec((1, d)), _const_spec(tables.shape),
                  _const_spec(w_in2.shape), _const_spec(g_cq.shape), _const_spec(w_uq2.shape),
                  _const_spec(g_ckv.shape), _const_spec(w_kv.shape)],
        out_specs=(heads(MLA_HEADS), heads(MLA_HEADS), heads(MLA_PAIRS))
        + tuple(split(dil) for _, dil in DIL_PATTERNS),
        out_shape=out_shapes,
        scratch_shapes=[pltpu.VMEM((2 * MXU_N // LANES, tm, LANES), F32)],
        compiler_params=pltpu.CompilerParams(
            dimension_semantics=("arbitrary", "arbitrary"), vmem_limit_bytes=VMEM_LIMIT),
        name="proj",
    )(x, pos, shift, scale, g_pre, tables, w_in2, g_cq, w_uq2, g_ckv, w_kv)


def _mla_kernel(q_ref, k_ref, v_ref, o_ref, s_even, s_odd, p_even, p_odd, acc_scr, *,
                tk, unroll):
    n_units, tq, _ = acc_scr.shape
    n_kv = k_ref.shape[1] // tk
    kv_bits = n_kv.bit_length() - 1
    head_bits = MLA_HEADS.bit_length() - 1
    assert n_kv == 1 << kv_bits and MLA_HEADS == 1 << head_bits and unroll % 2 == 0
    n_steps = n_units * n_kv
    lane = lax.broadcasted_iota(jnp.int32, (1, LANES), 1)
    ones = jnp.ones((tk, LANES), BF16)

    def split(t):
        unit = lax.shift_right_logical(t, kv_bits)
        return (unit, lax.shift_right_logical(unit, head_bits),
                lax.bitwise_and(unit, MLA_HEADS - 1), lax.bitwise_and(t, n_kv - 1))

    def scores(t, s_scr):
        _, qt, h, j = split(t)
        rows = pl.ds(pl.multiple_of(j * tk, tk), tk)
        q = q_ref[h, pl.ds(pl.multiple_of(qt * tq, tq), tq), :]
        s_scr[...] = _dot_nt(q, k_ref[h, rows, :])

    def softmax(t, m, s_scr, p_scr):
        j = split(t)[3]
        s = s_scr[...]
        m_prev = jnp.where(j == 0, NEG_INF, m)
        m_new = jnp.maximum(m_prev, jnp.max(s, axis=-1, keepdims=True))
        p_scr[...] = jnp.exp2(s - m_new).astype(BF16)
        return m_new, jnp.exp2(m_prev - m_new)

    def values(t, alpha, p_scr):
        unit, _, h, j = split(t)
        rows = pl.ds(pl.multiple_of(j * tk, tk), tk)
        rhs = jnp.concatenate([v_ref[lax.shift_right_logical(h, 1), rows, :], ones], axis=1)
        acc_scr[unit] = alpha * acc_scr[unit] + _dot(p_scr[...], rhs)

    s_buf, p_buf = (s_even, s_odd), (p_even, p_odd)

    def time_step(t, par, m, alpha_prev, with_scores=True):
        if with_scores:
            scores(t + 1, s_buf[1 - par])
        m, alpha = softmax(t, m, s_buf[par], p_buf[par])
        values(t - 1, alpha_prev, p_buf[1 - par])
        return m, alpha

    def body(u, carry):
        t0 = unroll * u + 1
        for i in range(unroll):
            carry = time_step(t0 + i, (1 + i) % 2, *carry)
        return carry

    @pl.when(jnp.logical_and(pl.program_id(0) == 0, pl.program_id(1) == 0))
    def _():
        acc_scr[...] = jnp.zeros(acc_scr.shape, F32)

    scores(jnp.int32(0), s_even)
    scores(jnp.int32(1), s_odd)
    carry = softmax(jnp.int32(0), jnp.full((tq, 1), NEG_INF, F32), s_even, p_even)
    n_iter = (n_steps - 2) // unroll
    carry = lax.fori_loop(0, n_iter, body, carry)
    for t in range(n_iter * unroll + 1, n_steps):
        carry = time_step(jnp.int32(t), t % 2, *carry, with_scores=t + 1 < n_steps)
    values(jnp.int32(n_steps - 1), carry[1], p_buf[(n_steps - 1) % 2])
    for qt in range(n_units // MLA_HEADS):
        for pair in range(MLA_PAIRS):
            halves = []
            for h in (2 * pair, 2 * pair + 1):
                acc = acc_scr[qt * MLA_HEADS + h]
                halves.append(acc[:, :LANES] / acc[:, LANES:])
            o_ref[pair, qt * tq:(qt + 1) * tq, :] = (
                jnp.where(lane < MLA_V, halves[0], halves[1]).astype(BF16))


def _mla(q, k, v, tq=512, q_tiles=1, tk=1024, unroll=6):
    bsz, nh, s, w = q.shape
    rows = tq * q_tiles
    return pl.pallas_call(
        functools.partial(_mla_kernel, tk=tk, unroll=unroll),
        grid=(bsz, s // rows),
        in_specs=[
            pl.BlockSpec((None, nh, rows, w), lambda b, i: (b, 0, i, 0)),
            pl.BlockSpec((None, nh, s, w), lambda b, i: (b, 0, 0, 0)),
            pl.BlockSpec((None, v.shape[1], s, w), lambda b, i: (b, 0, 0, 0)),
        ],
        out_specs=pl.BlockSpec((None, v.shape[1], rows, w), lambda b, i: (b, 0, i, 0)),
        out_shape=jax.ShapeDtypeStruct(v.shape, BF16),
        scratch_shapes=[pltpu.VMEM((tq, tk), F32), pltpu.VMEM((tq, tk), F32),
                        pltpu.VMEM((tq, tk), BF16), pltpu.VMEM((tq, tk), BF16),
                        pltpu.VMEM((q_tiles * nh, tq, 2 * LANES), F32)],
        compiler_params=pltpu.CompilerParams(
            dimension_semantics=("arbitrary", "arbitrary"), vmem_limit_bytes=VMEM_LIMIT),
        name="mla",
    )(q, k, v)


def _dil_kernel(q_ref, k_ref, v_ref, o_ref, lse_ref, *, n_side, sub):
    n_res, chunk, _ = q_ref.shape
    seq = k_ref.shape[1]
    win = sub + 2 * n_side
    base = 0 if chunk == seq else pl.program_id(2) * chunk
    rel = (lax.broadcasted_iota(jnp.int32, (sub, win), 1)
           - lax.broadcasted_iota(jnp.int32, (sub, win), 0))
    lane_head = lax.broadcasted_iota(jnp.int32, (1, DIL_GROUP_W), 1) // DIL_HEAD_DIM
    head_masks = [lane_head == h for h in range(DIL_HEADS)]

    def scores(r, t):
        qs = base + t * sub
        if isinstance(qs, int):
            ks = min(max(qs - n_side, 0), seq - win)
        else:
            ks = pl.multiple_of(jnp.clip(qs - n_side, 0, seq - win), n_side)
        q = q_ref[r, t * sub:(t + 1) * sub, :]
        kw = k_ref[r, pl.ds(ks, win), :]
        valid = jnp.abs(rel + (ks - qs)) <= n_side
        s = [jnp.where(valid, _dot_nt(jnp.where(hm, q, jnp.zeros_like(q)), kw), NEG_INF)
             for hm in head_masks]
        return r, t, ks, s

    def finish(r, t, ks, s):
        vw = v_ref[r, pl.ds(ks, win), :]
        m = [jnp.max(sh, axis=-1, keepdims=True) for sh in s]
        p = [jnp.exp2(sh - mh) for sh, mh in zip(s, m)]
        l = [jnp.sum(ph, axis=-1, keepdims=True) for ph in p]
        pv = [_dot(ph.astype(BF16), vw) for ph in p]
        o = jnp.zeros((sub, DIL_GROUP_W), F32)
        lse = jnp.zeros((sub, DIL_GROUP_W), F32)
        for hm, mh, lh, pvh in zip(head_masks, m, l, pv):
            o = jnp.where(hm, pvh / lh, o)
            lse = jnp.where(hm, mh * LN2 + jnp.log(lh), lse)
        o_ref[r, t * sub:(t + 1) * sub, :] = o.astype(BF16)
        lse_ref[r, t * sub:(t + 1) * sub, :] = lse

    tiles = [(r, t) for r in range(n_res) for t in range(chunk // sub)]
    pending = scores(*tiles[0])
    for nxt in tiles[1:]:
        upcoming = scores(*nxt)
        finish(*pending)
        pending = upcoming
    finish(*pending)


def _dilated(qkv, n_side, n_res, chunk, sub=128):
    bsz, dil, seq, _ = qkv.shape
    w = DIL_GROUP_W
    qspec = pl.BlockSpec((None, n_res, chunk, w), lambda b, r, i: (b, r, i, 0))
    kspec = pl.BlockSpec((None, n_res, seq, w), lambda b, r, i: (b, r, 0, 1))
    vspec = pl.BlockSpec((None, n_res, seq, w), lambda b, r, i: (b, r, 0, 2))
    out = (bsz, dil, seq, w)
    return pl.pallas_call(
        functools.partial(_dil_kernel, n_side=n_side, sub=sub),
        grid=(bsz, dil // n_res, seq // chunk),
        in_specs=[qspec, kspec, vspec],
        out_specs=(qspec, qspec),
        out_shape=(jax.ShapeDtypeStruct(out, BF16), jax.ShapeDtypeStruct(out, F32)),
        compiler_params=pltpu.CompilerParams(
            dimension_semantics=("arbitrary",) * 3, vmem_limit_bytes=VMEM_LIMIT),
        name="dilated",
    )(qkv, qkv, qkv)


def _post_kernel(x_ref, shift_ref, scale_ref, gate_ref, gpre_ref, gpost_ref, wgate_ref,
                 omla_ref, o0_ref, o1_ref, o2_ref, l0_ref, l1_ref, l2_ref,
                 womla_ref, wodil_ref, wout_ref, out_ref, merge_scr):
    tm = x_ref.shape[0]
    rows_per = tm // POST_ROW_GROUPS
    halves = DIL_GROUP_W // LANES

    def natural(ref, part, slot):
        dil = ref.shape[0]
        n = rows_per // dil
        if dil == 1:
            return ref[0, part * n:(part + 1) * n, :].astype(F32)
        for r in range(dil):
            blk = ref[r, part * n:(part + 1) * n, :].astype(F32)
            for hf in range(halves):
                merge_scr[slot * halves + hf, pl.ds(part * rows_per + r, n, stride=dil), :] = (
                    blk[:, hf * LANES:(hf + 1) * LANES])
        rows = slice(part * rows_per, (part + 1) * rows_per)
        return jnp.concatenate([merge_scr[slot * halves + hf, rows, :] for hf in range(halves)],
                               axis=-1)

    def mixed_dilated(part):
        l0, l1, l2 = natural(l0_ref, part, 0), natural(l1_ref, part, 1), natural(l2_ref, part, 2)
        mx = jnp.maximum(jnp.maximum(l0, l1), l2)
        e0, e1, e2 = jnp.exp(l0 - mx), jnp.exp(l1 - mx), jnp.exp(l2 - mx)
        inv = 1.0 / (e0 + e1 + e2)
        return ((e0 * inv) * natural(o0_ref, part, 3) + (e1 * inv) * natural(o1_ref, part, 4)
                + (e2 * inv) * natural(o2_ref, part, 5)).astype(BF16)

    parts = [slice(p * rows_per, (p + 1) * rows_per) for p in range(POST_ROW_GROUPS)]
    us = [(_rms(x_ref[rows, :], gpre_ref[...]) * (1.0 + scale_ref[...])
           + shift_ref[...]).astype(BF16) for rows in parts]

    def first_matmuls(part):
        rows = parts[part]
        o_mla = jnp.concatenate([omla_ref[p, rows, :] for p in range(MLA_PAIRS)], axis=-1)
        return (_dot(us[part], wgate_ref[:, :D_MODEL]), _dot(us[part], wgate_ref[:, D_MODEL:]),
                _dot(o_mla, womla_ref[...]))

    def merged(part, firsts):
        ga, gb, ma = firsts
        md = _dot(mixed_dilated(part), wodil_ref[...])
        return (jax.nn.sigmoid(ga) * ma + jax.nn.sigmoid(gb) * md).astype(BF16)

    def finish(part, y):
        rows = parts[part]
        out_ref[rows, :] = x_ref[rows, :] + gate_ref[...] * _rms(y, gpost_ref[...])

    firsts = first_matmuls(0)
    for part in range(POST_ROW_GROUPS):
        mg = merged(part, firsts)
        if part + 1 < POST_ROW_GROUPS:
            firsts = first_matmuls(part + 1)
        finish(part, _dot(mg, wout_ref[...]))


def _post(x, shift, scale, gate, g_pre, g_post, w_gates, o_mla, o_dil, lse_dil,
          w_o_mla, w_o_dil, w_out, tm=1024):
    bsz, s, d = x.shape
    tok = lambda n: pl.BlockSpec((None, tm, n), lambda b, i: (b, i, 0))
    mod = pl.BlockSpec((None, 1, d), lambda b, i: (b, 0, 0))
    gw = DIL_GROUP_W
    split = [pl.BlockSpec((None, dil, tm // dil, gw), lambda b, i: (b, 0, i, 0))
             for _, dil in DIL_PATTERNS]
    return pl.pallas_call(
        _post_kernel,
        grid=(bsz, s // tm),
        in_specs=[tok(d), mod, mod, mod, _const_spec((1, d)), _const_spec((1, d)),
                  _const_spec(w_gates.shape),
                  pl.BlockSpec((None, MLA_PAIRS, tm, LANES), lambda b, i: (b, 0, i, 0)),
                  *split, *split,
                  _const_spec(w_o_mla.shape), _const_spec(w_o_dil.shape),
                  _const_spec(w_out.shape)],
        out_specs=tok(d),
        out_shape=jax.ShapeDtypeStruct(x.shape, F32),
        scratch_shapes=[pltpu.VMEM((2 * N_DIL_GROUPS * gw // LANES, tm, LANES), F32)],
        compiler_params=pltpu.CompilerParams(
            dimension_semantics=("arbitrary", "arbitrary"), vmem_limit_bytes=VMEM_LIMIT),
        name="post",
    )(x, shift, scale, gate, g_pre, g_post, w_gates, o_mla, *o_dil, *lse_dil,
      w_o_mla, w_o_dil, w_out)


def _tables():
    t = np.zeros((N_TABLE_ROWS, LANES), np.float32)
    scale = np.float32((MLA_NOPE + MLA_ROPE) ** -0.5 * LOG2E)
    half = MLA_ROPE // 2
    f_m = MLA_THETA ** (-np.arange(0, MLA_ROPE, 2, dtype=np.float32) / MLA_ROPE)
    f_d = ROPE_THETA ** (-np.arange(0, ROPE_PART, 2, dtype=np.float32) / ROPE_PART)
    t[ROW_INVF, :ROPE_PART] = np.tile(f_d, 2)
    t[ROW_INVF, LANES // 2:] = np.tile(f_m, 4)
    t[ROW_ONE_Q, :MLA_NOPE] = scale
    t[ROW_MC_Q, MLA_NOPE:MLA_NOPE + MLA_ROPE] = scale
    t[ROW_MS_Q, MLA_NOPE + MLA_ROPE:MLA_NOPE + MLA_ROPE + half] = -scale
    t[ROW_MS_Q, MLA_NOPE + MLA_ROPE + half:] = scale
    t[ROW_MC_K, :MLA_ROPE] = 1.0
    t[ROW_MS_K, MLA_ROPE:MLA_ROPE + half] = -1.0
    t[ROW_MS_K, MLA_ROPE + half:2 * MLA_ROPE] = 1.0
    hp = ROPE_PART // 2
    for off in (0, DIL_HEAD_DIM):
        t[ROW_M2_D, off:off + hp] = -1.0
        t[ROW_M1_D, off + hp:off + ROPE_PART] = 1.0
    return jnp.asarray(t)


def _layout_weights(w_in, w_uq, w_ukv):
    o_ckv = MLA_Q_LORA
    o_kr = o_ckv + MLA_KV_LORA
    o_qkv = o_kr + MLA_ROPE
    o_ga = o_qkv + DIL_QKV
    half = MLA_ROPE // 2
    k_raw = w_in[:, o_kr:o_qkv]
    k_swap = jnp.concatenate([k_raw[:, half:], k_raw[:, :half]], axis=1)
    pad = jnp.zeros((D_MODEL, 2 * MLA_Q_LORA - MLA_Q_LORA - MLA_KV_LORA - 2 * MLA_ROPE), w_in.dtype)
    w_in2 = jnp.concatenate(
        [w_in[:, :o_kr], k_raw, k_swap, pad, w_in[:, o_qkv:o_ga]], axis=1).astype(BF16)
    w_gates = w_in[:, o_ga:].astype(BF16)

    wq = w_uq.reshape(MLA_Q_LORA, MLA_HEADS, MLA_NOPE + MLA_ROPE)
    nope, t1, t2 = wq[..., :MLA_NOPE], wq[..., MLA_NOPE:MLA_NOPE + half], wq[..., MLA_NOPE + half:]
    w_uq2 = jnp.concatenate([nope, t1, t2, t2, t1], axis=-1).reshape(
        MLA_Q_LORA, MLA_HEADS * MLA_PAD).astype(BF16)

    wkv = w_ukv.reshape(MLA_KV_LORA, MLA_HEADS, MLA_NOPE + MLA_V)
    k_cols = jnp.concatenate(
        [wkv[..., :MLA_NOPE], jnp.zeros((MLA_KV_LORA, MLA_HEADS, MLA_PAD - MLA_NOPE), w_ukv.dtype)],
        axis=-1).reshape(MLA_KV_LORA, MLA_HEADS * MLA_PAD)
    v_cols = wkv[..., MLA_NOPE:].reshape(MLA_KV_LORA, MLA_HEADS * MLA_V)
    place = np.zeros((MLA_ROPE, MLA_HEADS, MLA_PAD), np.float32)
    for i in range(MLA_ROPE):
        place[i, :, MLA_NOPE + i] = 1.0
        place[i, :, MLA_NOPE + MLA_ROPE + i] = 1.0
    place = jnp.asarray(place.reshape(MLA_ROPE, MLA_HEADS * MLA_PAD))
    n_lhs = 2 * MLA_Q_LORA - MLA_Q_LORA
    k_rows = jnp.concatenate(
        [k_cols, place, place,
         jnp.zeros((n_lhs - MLA_KV_LORA - 2 * MLA_ROPE, MLA_HEADS * MLA_PAD), F32)], axis=0)
    v_rows = jnp.concatenate(
        [v_cols, jnp.zeros((n_lhs - MLA_KV_LORA, MLA_HEADS * MLA_V), F32)], axis=0)
    w_kv = jnp.concatenate([k_rows, v_rows], axis=1).astype(BF16)
    return w_in2, w_gates, w_uq2, w_kv


def _ff_weights(w_gate, w_up, w_down):
    return w_gate.astype(BF16), w_up.astype(BF16), w_down.astype(BF16)


def kernel(x, c, positions, w_ada, b_ada, g_pre_ff1, w_gate1, w_up1, w_down1, g_post_ff1, g_pre_mix, w_in, g_cq, w_uq, g_ckv, w_ukv, w_o_mla, w_o_dil, w_out, g_post_mix, g_pre_ff2, w_gate2, w_up2, w_down2, g_post_ff2):
    bsz, s, d = x.shape
    depth = w_ada.shape[0]
    pos = positions.astype(F32).reshape(bsz, s, 1)
    tables = _tables()
    for l in range(depth):
        ada = _ada(c, w_ada[l], b_ada[l]).reshape(bsz, 3, 3, 1, d)
        mod = lambda i, j: ada[:, i, j]

        x = _ffn(x, mod(0, 0), mod(0, 1), mod(0, 2), g_pre_ff1[l][None], g_post_ff1[l][None],
                 *_ff_weights(w_gate1[l], w_up1[l], w_down1[l]))

        w_in2, w_gates, w_uq2, w_kv = _layout_weights(w_in[l], w_uq[l], w_ukv[l])
        q_m, k_m, v_m, *dil_qkv = _proj(x, pos, mod(1, 0), mod(1, 1), g_pre_mix[l][None], tables,
                                        w_in2, g_cq[l][None], w_uq2, g_ckv[l][None], w_kv)
        o_mla = _mla(q_m, k_m, v_m)

        outs, lses = [], []
        for qkv_g, (window, dil) in zip(dil_qkv, DIL_PATTERNS):
            chunk = min(s // dil, DIL_STEP_QUERIES)
            o_g, lse_g = _dilated(qkv_g, n_side=window // (2 * dil),
                                  n_res=min(dil, DIL_STEP_QUERIES // chunk), chunk=chunk)
            outs.append(o_g)
            lses.append(lse_g)

        x = _post(x, mod(1, 0), mod(1, 1), mod(1, 2), g_pre_mix[l][None], g_post_mix[l][None],
                  w_gates, o_mla, outs, lses,
                  w_o_mla[l].astype(BF16), w_o_dil[l].astype(BF16), w_out[l].astype(BF16))

        x = _ffn(x, mod(2, 0), mod(2, 1), mod(2, 2), g_pre_ff2[l][None], g_post_ff2[l][None],
                 *_ff_weights(w_gate2[l], w_up2[l], w_down2[l]))
    return x
```

```python
import functools

import numpy as np
import jax
import jax.numpy as jnp
from jax import lax
from jax.experimental import pallas as pl
from jax.experimental.pallas import tpu as pltpu

F32 = jnp.float32
BF16 = jnp.bfloat16

D_MODEL = 1024
D_FF = 2816
FFN_RES = 0.5
MLA_HEADS = 8
MLA_Q_LORA = 256
MLA_KV_LORA = 128
MLA_NOPE = 64
MLA_ROPE = 32
MLA_V = 64
MLA_THETA = 10000.0
DIL_PATTERNS = ((128, 1), (512, 4), (2048, 16))
N_DIL_GROUPS = 3
DIL_HEADS = 4
DIL_HEAD_DIM = 64
DIL_GROUP_W = DIL_HEADS * DIL_HEAD_DIM
DIL_QKV = 3 * N_DIL_GROUPS * DIL_GROUP_W
ROPE_THETA = 500000.0
ROPE_PART = DIL_HEAD_DIM // 4
NORM_EPS = 1e-6
NEG_INF = -1e30

LANES = 128
MXU_N = 256
FF_CHUNK = MXU_N
N_FF_CHUNKS = D_FF // FF_CHUNK
MLA_PAD = LANES
MLA_PAIRS = MLA_HEADS * MLA_V // LANES
LOG2E = float(np.log2(np.e))
LN2 = float(np.log(2.0))
DIL_STEP_QUERIES = 1024
FFN_ROW_GROUPS = 2
POST_ROW_GROUPS = 2
PROJ_ROW_GROUPS = 1
VMEM_LIMIT = 56 * 1024 * 1024

ROW_INVF, ROW_ONE_Q, ROW_MC_Q, ROW_MS_Q, ROW_MC_K, ROW_MS_K, ROW_M1_D, ROW_M2_D = range(8)
N_TABLE_ROWS = 8


def _rms(x, g):
    return x * lax.rsqrt(jnp.mean(x * x, axis=-1, keepdims=True) + NORM_EPS) * g


def _silu(x):
    return x * jax.nn.sigmoid(x)


def _dot(a, b):
    return jnp.dot(a, b, preferred_element_type=F32)


def _dot_nt(a, b):
    return lax.dot_general(a, b, (((1,), (1,)), ((), ())), preferred_element_type=F32)


def _const_spec(shape):
    nd = len(shape)
    return pl.BlockSpec(shape, lambda *_: (0,) * nd, pipeline_mode=pl.Buffered(1))


def _ada_kernel(c_ref, w_ref, b_ref, o_ref):
    cond = _silu(c_ref[...])
    o_ref[...] = _dot(cond, w_ref[...]) + b_ref[...]


def _ada(c, w, b, layer):
    bsz, d = c.shape
    depth, _, n = w.shape
    tn = 1024
    return pl.pallas_call(
        _ada_kernel,
        grid=(n // tn,),
        in_specs=[
            pl.BlockSpec((bsz, d), lambda j: (0, 0)),
            pl.BlockSpec((None, d, tn), lambda j: (layer, 0, j)),
            pl.BlockSpec((None, 1, tn), lambda j: (layer, 0, j)),
        ],
        out_specs=pl.BlockSpec((bsz, tn), lambda j: (0, j)),
        out_shape=jax.ShapeDtypeStruct((bsz, n), F32),
        compiler_params=pltpu.CompilerParams(dimension_semantics=("arbitrary",)),
        name="ada",
    )(c, w, b.reshape(depth, 1, n))


def _ffn_kernel(x_ref, shift_ref, scale_ref, gate_ref, gpre_ref, gpost_ref,
                wg_ref, wu_ref, wd_ref, o_ref):
    rows_per = x_ref.shape[0] // FFN_ROW_GROUPS
    parts = [slice(part * rows_per, (part + 1) * rows_per) for part in range(FFN_ROW_GROUPS)]
    hbs = [(_rms(x_ref[rows, :], gpre_ref[...]) * (1.0 + scale_ref[...])
            + shift_ref[...]).astype(BF16) for rows in parts]
    accs = [None] * FFN_ROW_GROUPS

    def gate_up(part, c):
        cols = slice(c * FF_CHUNK, (c + 1) * FF_CHUNK)
        return part, c, _dot(hbs[part], wg_ref[:, cols]), _dot(hbs[part], wu_ref[:, cols])

    def down(part, c, g, u):
        a = (_silu(g) * u).astype(BF16)
        d = _dot(a, wd_ref[c * FF_CHUNK:(c + 1) * FF_CHUNK, :])
        accs[part] = d if accs[part] is None else accs[part] + d
        if c == N_FF_CHUNKS - 1:
            rows = parts[part]
            y = _rms(accs[part], gpost_ref[...])
            o_ref[rows, :] = x_ref[rows, :] + (FFN_RES * gate_ref[...]) * y

    items = [(part, c) for part in range(FFN_ROW_GROUPS) for c in range(N_FF_CHUNKS)]
    pending = gate_up(*items[0])
    for nxt in items[1:]:
        upcoming = gate_up(*nxt)
        down(*pending)
        pending = upcoming
    down(*pending)


def _ffn(x, shift, scale, gate, g_pre, g_post, wg, wu, wd, tm=1024):
    bsz, s, d = x.shape
    tok = pl.BlockSpec((None, tm, d), lambda b, i: (b, i, 0))
    mod = pl.BlockSpec((None, 1, d), lambda b, i: (b, 0, 0))
    return pl.pallas_call(
        _ffn_kernel,
        grid=(bsz, s // tm),
        in_specs=[tok, mod, mod, mod, _const_spec((1, d)), _const_spec((1, d)),
                  _const_spec(wg.shape), _const_spec(wu.shape), _const_spec(wd.shape)],
        out_specs=tok,
        out_shape=jax.ShapeDtypeStruct(x.shape, F32),
        compiler_params=pltpu.CompilerParams(
            dimension_semantics=("arbitrary", "arbitrary"), vmem_limit_bytes=VMEM_LIMIT),
        name="ffn",
    )(x, shift, scale, gate, g_pre, g_post, wg, wu, wd)


def _proj_kernel(x_ref, pos_ref, shift_ref, scale_ref, gpre_ref, tab_ref, win_ref,
                 gcq_ref, wuq_ref, gckv_ref, wkv_ref,
                 q_ref, k_ref, v_ref, d0_ref, d1_ref, d2_ref, split_scr):
    d_refs = (d0_ref, d1_ref, d2_ref)
    rows_per = x_ref.shape[0] // PROJ_ROW_GROUPS
    q_scale = DIL_HEAD_DIM ** -0.5 * LOG2E
    base = 2 * MLA_Q_LORA
    n_chunks = DIL_QKV // MXU_N
    rope_chunks = 2 * N_DIL_GROUPS * DIL_GROUP_W // MXU_N
    q_chunks = N_DIL_GROUPS * DIL_GROUP_W // MXU_N
    halves = MXU_N // LANES
    low_half = lax.broadcasted_iota(jnp.int32, (1, LANES), 1) < LANES // 2

    def row(r):
        return tab_ref[r:r + 1, :]

    def part_stages(part, u):
        rows = slice(part * rows_per, (part + 1) * rows_per)
        ang = pos_ref[rows, :] * row(ROW_INVF)
        cos_a, sin_a = jnp.cos(ang), jnp.sin(ang)
        cos_r, sin_r = pltpu.roll(cos_a, LANES // 2, 1), pltpu.roll(sin_a, LANES // 2, 1)
        held = {}

        def post_cq(c_q):
            held["cq"] = _rms(c_q, gcq_ref[...]).astype(BF16)

        def post_q(q_pad):
            t_q = row(ROW_ONE_Q) + cos_a * row(ROW_MC_Q) + sin_a * row(ROW_MS_Q)
            for h in range(MLA_HEADS):
                q_ref[h, rows, :] = (q_pad[:, h * MLA_PAD:(h + 1) * MLA_PAD] * t_q).astype(BF16)

        def post_ckv(pb):
            c_kv = pb[:, :MLA_KV_LORA]
            r_kv = lax.rsqrt(jnp.mean(c_kv * c_kv, axis=-1, keepdims=True) + NORM_EPS)
            t_k = cos_r * row(ROW_MC_K) + sin_r * row(ROW_MS_K)
            held["kv_lhs"] = jnp.concatenate(
                [c_kv * r_kv * gckv_ref[...], pb[:, MLA_KV_LORA:] * t_k], axis=-1).astype(BF16)

        def post_kv(kv):
            for h in range(MLA_HEADS):
                k_ref[h, rows, :] = kv[:, h * MLA_PAD:(h + 1) * MLA_PAD].astype(BF16)
            v_off = MLA_HEADS * MLA_PAD
            for p in range(MLA_PAIRS):
                v_ref[p, rows, :] = kv[:, v_off + p * LANES: v_off + (p + 1) * LANES].astype(BF16)

        cd = jnp.where(low_half, cos_a, cos_r)
        sd = jnp.where(low_half, sin_a, sin_r)
        s1 = sd * row(ROW_M1_D)
        s2 = sd * row(ROW_M2_D)

        def post_dil(j, pc):
            which, g = divmod(j, N_DIL_GROUPS)
            dil = DIL_PATTERNS[g][1]
            d_ref = d_refs[g]
            n = rows_per // dil
            out_rows = slice(part * n, (part + 1) * n)
            for half in range(halves):
                xs = pc[:, half * LANES:(half + 1) * LANES]
                if j < rope_chunks:
                    xs = (xs * cd + pltpu.roll(xs, ROPE_PART // 2, 1) * s1
                          + pltpu.roll(xs, LANES - ROPE_PART // 2, 1) * s2)
                    if j < q_chunks:
                        xs = xs * q_scale
                lo = which * DIL_GROUP_W + half * LANES
                if dil == 1:
                    d_ref[0, out_rows, lo:lo + LANES] = xs.astype(BF16)
                else:
                    buf = (part * 2 + j % 2) * halves + half
                    split_scr[buf] = xs
                    for r in range(dil):
                        d_ref[r, out_rows, lo:lo + LANES] = (
                            split_scr[buf, pl.ds(r, n, stride=dil), :].astype(BF16))

        def w_in_cols(lo):
            return lambda: _dot(u, win_ref[:, lo:lo + MXU_N])

        return [
            (w_in_cols(0), post_cq),
            (w_in_cols(MLA_Q_LORA), post_ckv),
            (w_in_cols(base), functools.partial(post_dil, 0)),
            (lambda: _dot(held["cq"], wuq_ref[...]), post_q),
            (lambda: _dot(held["kv_lhs"], wkv_ref[...]), post_kv),
        ] + [(w_in_cols(base + j * MXU_N), functools.partial(post_dil, j))
             for j in range(1, n_chunks)]

    us = [(_rms(x_ref[part * rows_per:(part + 1) * rows_per, :], gpre_ref[...])
           * (1.0 + scale_ref[...]) + shift_ref[...]).astype(BF16)
          for part in range(PROJ_ROW_GROUPS)]
    stages = [st for part, u in enumerate(us) for st in part_stages(part, u)]
    result = stages[0][0]()
    for i, (_, post) in enumerate(stages):
        upcoming = stages[i + 1][0]() if i + 1 < len(stages) else None
        post(result)
        result = upcoming


def _proj(x, pos, shift, scale, g_pre, tables, w_in2, g_cq, w_uq2, g_ckv, w_kv, tm=512):
    bsz, s, d = x.shape
    tok = lambda n: pl.BlockSpec((None, tm, n), lambda b, i: (b, i, 0))
    mod = pl.BlockSpec((None, 1, d), lambda b, i: (b, 0, 0))
    heads = lambda n: pl.BlockSpec((None, n, tm, LANES), lambda b, i: (b, 0, i, 0))
    out_shapes = (
        jax.ShapeDtypeStruct((bsz, MLA_HEADS, s, MLA_PAD), BF16),
        jax.ShapeDtypeStruct((bsz, MLA_HEADS, s, MLA_PAD), BF16),
        jax.ShapeDtypeStruct((bsz, MLA_PAIRS, s, LANES), BF16),
    ) + tuple(jax.ShapeDtypeStruct((bsz, dil, s // dil, 3 * DIL_GROUP_W), BF16)
              for _, dil in DIL_PATTERNS)
    split = lambda dil: pl.BlockSpec((None, dil, tm // dil, 3 * DIL_GROUP_W),
                                     lambda b, i: (b, 0, i, 0))
    return pl.pallas_call(
        _proj_kernel,
        grid=(bsz, s // tm),
        in_specs=[tok(d), tok(1), mod, mod, _const_spec((1, d)), _const_spec(tables.shape),
                  _const_spec(w_in2.shape), _const_spec(g_cq.shape), _const_spec(w_uq2.shape),
                  _const_spec(g_ckv.shape), _const_spec(w_kv.shape)],
        out_specs=(heads(MLA_HEADS), heads(MLA_HEADS), heads(MLA_PAIRS))
        + tuple(split(dil) for _, dil in DIL_PATTERNS),
        out_shape=out_shapes,
        scratch_shapes=[pltpu.VMEM((PROJ_ROW_GROUPS * 2 * MXU_N // LANES,
                                    tm // PROJ_ROW_GROUPS, LANES), F32)],
        compiler_params=pltpu.CompilerParams(
            dimension_semantics=("arbitrary", "arbitrary"), vmem_limit_bytes=VMEM_LIMIT),
        name="proj",
    )(x, pos, shift, scale, g_pre, tables, w_in2, g_cq, w_uq2, g_ckv, w_kv)


def _mla_kernel(q_ref, k_ref, v_ref, o_ref, s_even, s_odd, p_even, p_odd, acc_scr, *,
                tk, unroll):
    n_units, tq, _ = acc_scr.shape
    n_kv = k_ref.shape[1] // tk
    kv_bits = n_kv.bit_length() - 1
    head_bits = MLA_HEADS.bit_length() - 1
    assert n_kv == 1 << kv_bits and MLA_HEADS == 1 << head_bits and unroll % 2 == 0
    n_steps = n_units * n_kv
    lane = lax.broadcasted_iota(jnp.int32, (1, LANES), 1)
    ones = jnp.ones((tk, LANES), BF16)

    def split(t):
        unit = lax.shift_right_logical(t, kv_bits)
        return (unit, lax.shift_right_logical(unit, head_bits),
                lax.bitwise_and(unit, MLA_HEADS - 1), lax.bitwise_and(t, n_kv - 1))

    def scores(t, s_scr):
        _, qt, h, j = split(t)
        rows = pl.ds(pl.multiple_of(j * tk, tk), tk)
        q = q_ref[h, pl.ds(pl.multiple_of(qt * tq, tq), tq), :]
        s_scr[...] = _dot_nt(q, k_ref[h, rows, :])

    def softmax(t, m, s_scr, p_scr):
        j = split(t)[3]
        s = s_scr[...]
        m_prev = jnp.where(j == 0, NEG_INF, m)
        m_new = jnp.maximum(m_prev, jnp.max(s, axis=-1, keepdims=True))
        p_scr[...] = jnp.exp2(s - m_new).astype(BF16)
        return m_new, jnp.exp2(m_prev - m_new)

    def values(t, alpha, p_scr):
        unit, _, h, j = split(t)
        rows = pl.ds(pl.multiple_of(j * tk, tk), tk)
        rhs = jnp.concatenate([v_ref[lax.shift_right_logical(h, 1), rows, :], ones], axis=1)
        acc_scr[unit] = alpha * acc_scr[unit] + _dot(p_scr[...], rhs)

    s_buf, p_buf = (s_even, s_odd), (p_even, p_odd)

    def time_step(t, par, m, alpha_prev, with_scores=True):
        if with_scores:
            scores(t + 1, s_buf[1 - par])
        m, alpha = softmax(t, m, s_buf[par], p_buf[par])
        values(t - 1, alpha_prev, p_buf[1 - par])
        return m, alpha

    def body(u, carry):
        t0 = unroll * u + 1
        for i in range(unroll):
            carry = time_step(t0 + i, (1 + i) % 2, *carry)
        return carry

    @pl.when(jnp.logical_and(pl.program_id(0) == 0, pl.program_id(1) == 0))
    def _():
        acc_scr[...] = jnp.zeros(acc_scr.shape, F32)

    scores(jnp.int32(0), s_even)
    scores(jnp.int32(1), s_odd)
    carry = softmax(jnp.int32(0), jnp.full((tq, 1), NEG_INF, F32), s_even, p_even)
    n_iter = (n_steps - 2) // unroll
    carry = lax.fori_loop(0, n_iter, body, carry)
    for t in range(n_iter * unroll + 1, n_steps):
        carry = time_step(jnp.int32(t), t % 2, *carry, with_scores=t + 1 < n_steps)
    values(jnp.int32(n_steps - 1), carry[1], p_buf[(n_steps - 1) % 2])
    for qt in range(n_units // MLA_HEADS):
        for pair in range(MLA_PAIRS):
            halves = []
            for h in (2 * pair, 2 * pair + 1):
                acc = acc_scr[qt * MLA_HEADS + h]
                halves.append(acc[:, :LANES] / acc[:, LANES:])
            o_ref[pair, qt * tq:(qt + 1) * tq, :] = (
                jnp.where(lane < MLA_V, halves[0], halves[1]).astype(BF16))


def _mla(q, k, v, tq=512, q_tiles=1, tk=1024, unroll=30):
    bsz, nh, s, w = q.shape
    rows = tq * q_tiles
    return pl.pallas_call(
        functools.partial(_mla_kernel, tk=tk, unroll=unroll),
        grid=(bsz, s // rows),
        in_specs=[
            pl.BlockSpec((None, nh, rows, w), lambda b, i: (b, 0, i, 0)),
            pl.BlockSpec((None, nh, s, w), lambda b, i: (b, 0, 0, 0)),
            pl.BlockSpec((None, v.shape[1], s, w), lambda b, i: (b, 0, 0, 0)),
        ],
        out_specs=pl.BlockSpec((None, v.shape[1], rows, w), lambda b, i: (b, 0, i, 0)),
        out_shape=jax.ShapeDtypeStruct(v.shape, BF16),
        scratch_shapes=[pltpu.VMEM((tq, tk), F32), pltpu.VMEM((tq, tk), F32),
                        pltpu.VMEM((tq, tk), BF16), pltpu.VMEM((tq, tk), BF16),
                        pltpu.VMEM((q_tiles * nh, tq, 2 * LANES), F32)],
        compiler_params=pltpu.CompilerParams(
            dimension_semantics=("arbitrary", "arbitrary"), vmem_limit_bytes=VMEM_LIMIT),
        name="mla",
    )(q, k, v)


def _dil_kernel(q_ref, k_ref, v_ref, o_ref, lse_ref, *, n_side, sub):
    n_res, chunk, _ = q_ref.shape
    seq = k_ref.shape[1]
    win = min(seq, sub + 2 * n_side)
    base = 0 if chunk == seq else pl.program_id(2) * chunk
    rel = (lax.broadcasted_iota(jnp.int32, (sub, win), 1)
           - lax.broadcasted_iota(jnp.int32, (sub, win), 0))
    lane_head = lax.broadcasted_iota(jnp.int32, (1, DIL_GROUP_W), 1) // DIL_HEAD_DIM
    head_masks = [lane_head == h for h in range(DIL_HEADS)]

    def scores(r, t):
        qs = base + t * sub
        if isinstance(qs, int):
            ks = min(max(qs - n_side, 0), seq - win)
        else:
            ks = pl.multiple_of(jnp.clip(qs - n_side, 0, seq - win), n_side)
        q = q_ref[r, t * sub:(t + 1) * sub, :]
        kw = k_ref[r, pl.ds(ks, win), :]
        valid = jnp.abs(rel + (ks - qs)) <= n_side
        s = [jnp.where(valid, _dot_nt(jnp.where(hm, q, jnp.zeros_like(q)), kw), NEG_INF)
             for hm in head_masks]
        return r, t, ks, s

    def finish(r, t, ks, s):
        vw = v_ref[r, pl.ds(ks, win), :]
        m = [jnp.max(sh, axis=-1, keepdims=True) for sh in s]
        p = [jnp.exp2(sh - mh) for sh, mh in zip(s, m)]
        l = [jnp.sum(ph, axis=-1, keepdims=True) for ph in p]
        pv = [_dot(ph.astype(BF16), vw) for ph in p]
        o = jnp.zeros((sub, DIL_GROUP_W), F32)
        lse = jnp.zeros((sub, DIL_GROUP_W), F32)
        for hm, mh, lh, pvh in zip(head_masks, m, l, pv):
            o = jnp.where(hm, pvh / lh, o)
            lse = jnp.where(hm, mh * LN2 + jnp.log(lh), lse)
        o_ref[r, t * sub:(t + 1) * sub, :] = o.astype(BF16)
        lse_ref[r, t * sub:(t + 1) * sub, :] = lse

    tiles = [(r, t) for r in range(n_res) for t in range(chunk // sub)]
    pending = scores(*tiles[0])
    for nxt in tiles[1:]:
        upcoming = scores(*nxt)
        finish(*pending)
        pending = upcoming
    finish(*pending)


def _dilated(qkv, n_side, n_res, chunk):
    bsz, dil, seq, _ = qkv.shape
    sub = seq if seq <= 4 * n_side else 2 * n_side
    w = DIL_GROUP_W
    qspec = pl.BlockSpec((None, n_res, chunk, w), lambda b, r, i: (b, r, i, 0))
    kspec = pl.BlockSpec((None, n_res, seq, w), lambda b, r, i: (b, r, 0, 1))
    vspec = pl.BlockSpec((None, n_res, seq, w), lambda b, r, i: (b, r, 0, 2))
    out = (bsz, dil, seq, w)
    return pl.pallas_call(
        functools.partial(_dil_kernel, n_side=n_side, sub=sub),
        grid=(bsz, dil // n_res, seq // chunk),
        in_specs=[qspec, kspec, vspec],
        out_specs=(qspec, qspec),
        out_shape=(jax.ShapeDtypeStruct(out, BF16), jax.ShapeDtypeStruct(out, F32)),
        compiler_params=pltpu.CompilerParams(
            dimension_semantics=("arbitrary",) * 3, vmem_limit_bytes=VMEM_LIMIT),
        name="dilated",
    )(qkv, qkv, qkv)


def _post_kernel(x_ref, shift_ref, scale_ref, gate_ref, gpre_ref, gpost_ref, wgate_ref,
                 omla_ref, o0_ref, o1_ref, o2_ref, l0_ref, l1_ref, l2_ref,
                 womla_ref, wodil_ref, wout_ref, out_ref, merge_scr):
    tm = x_ref.shape[0]
    rows_per = tm // POST_ROW_GROUPS
    halves = DIL_GROUP_W // LANES

    def natural(ref, part, slot):
        dil = ref.shape[0]
        n = rows_per // dil
        if dil == 1:
            return ref[0, part * n:(part + 1) * n, :].astype(F32)
        for r in range(dil):
            blk = ref[r, part * n:(part + 1) * n, :].astype(F32)
            for hf in range(halves):
                merge_scr[slot * halves + hf, pl.ds(part * rows_per + r, n, stride=dil), :] = (
                    blk[:, hf * LANES:(hf + 1) * LANES])
        rows = slice(part * rows_per, (part + 1) * rows_per)
        return jnp.concatenate([merge_scr[slot * halves + hf, rows, :] for hf in range(halves)],
                               axis=-1)

    def mixed_dilated(part):
        l0, l1, l2 = natural(l0_ref, part, 0), natural(l1_ref, part, 1), natural(l2_ref, part, 2)
        mx = jnp.maximum(jnp.maximum(l0, l1), l2)
        e0, e1, e2 = jnp.exp(l0 - mx), jnp.exp(l1 - mx), jnp.exp(l2 - mx)
        inv = 1.0 / (e0 + e1 + e2)
        return ((e0 * inv) * natural(o0_ref, part, 3) + (e1 * inv) * natural(o1_ref, part, 4)
                + (e2 * inv) * natural(o2_ref, part, 5)).astype(BF16)

    parts = [slice(p * rows_per, (p + 1) * rows_per) for p in range(POST_ROW_GROUPS)]
    us = [(_rms(x_ref[rows, :], gpre_ref[...]) * (1.0 + scale_ref[...])
           + shift_ref[...]).astype(BF16) for rows in parts]

    def first_matmuls(part):
        rows = parts[part]
        o_mla = jnp.concatenate([omla_ref[p, rows, :] for p in range(MLA_PAIRS)], axis=-1)
        return (_dot(us[part], wgate_ref[:, :D_MODEL]), _dot(us[part], wgate_ref[:, D_MODEL:]),
                _dot(o_mla, womla_ref[...]))

    def merged(part, firsts):
        ga, gb, ma = firsts
        md = _dot(mixed_dilated(part), wodil_ref[...])
        return (jax.nn.sigmoid(ga) * ma + jax.nn.sigmoid(gb) * md).astype(BF16)

    def finish(part, y):
        rows = parts[part]
        out_ref[rows, :] = x_ref[rows, :] + gate_ref[...] * _rms(y, gpost_ref[...])

    firsts = first_matmuls(0)
    for part in range(POST_ROW_GROUPS):
        mg = merged(part, firsts)
        if part + 1 < POST_ROW_GROUPS:
            firsts = first_matmuls(part + 1)
        finish(part, _dot(mg, wout_ref[...]))


def _post(x, shift, scale, gate, g_pre, g_post, w_gates, o_mla, o_dil, lse_dil,
          w_o_mla, w_o_dil, w_out, tm=1024):
    bsz, s, d = x.shape
    tok = lambda n: pl.BlockSpec((None, tm, n), lambda b, i: (b, i, 0))
    mod = pl.BlockSpec((None, 1, d), lambda b, i: (b, 0, 0))
    gw = DIL_GROUP_W
    split = [pl.BlockSpec((None, dil, tm // dil, gw), lambda b, i: (b, 0, i, 0))
             for _, dil in DIL_PATTERNS]
    return pl.pallas_call(
        _post_kernel,
        grid=(bsz, s // tm),
        in_specs=[tok(d), mod, mod, mod, _const_spec((1, d)), _const_spec((1, d)),
                  _const_spec(w_gates.shape),
                  pl.BlockSpec((None, MLA_PAIRS, tm, LANES), lambda b, i: (b, 0, i, 0)),
                  *split, *split,
                  _const_spec(w_o_mla.shape), _const_spec(w_o_dil.shape),
                  _const_spec(w_out.shape)],
        out_specs=tok(d),
        out_shape=jax.ShapeDtypeStruct(x.shape, F32),
        scratch_shapes=[pltpu.VMEM((2 * N_DIL_GROUPS * gw // LANES, tm, LANES), F32)],
        compiler_params=pltpu.CompilerParams(
            dimension_semantics=("arbitrary", "arbitrary"), vmem_limit_bytes=VMEM_LIMIT),
        name="post",
    )(x, shift, scale, gate, g_pre, g_post, w_gates, o_mla, *o_dil, *lse_dil,
      w_o_mla, w_o_dil, w_out)


def _tables():
    t = np.zeros((N_TABLE_ROWS, LANES), np.float32)
    scale = np.float32((MLA_NOPE + MLA_ROPE) ** -0.5 * LOG2E)
    half = MLA_ROPE // 2
    f_m = MLA_THETA ** (-np.arange(0, MLA_ROPE, 2, dtype=np.float32) / MLA_ROPE)
    f_d = ROPE_THETA ** (-np.arange(0, ROPE_PART, 2, dtype=np.float32) / ROPE_PART)
    t[ROW_INVF, :ROPE_PART] = np.tile(f_d, 2)
    t[ROW_INVF, LANES // 2:] = np.tile(f_m, 4)
    t[ROW_ONE_Q, :MLA_NOPE] = scale
    t[ROW_MC_Q, MLA_NOPE:MLA_NOPE + MLA_ROPE] = scale
    t[ROW_MS_Q, MLA_NOPE + MLA_ROPE:MLA_NOPE + MLA_ROPE + half] = -scale
    t[ROW_MS_Q, MLA_NOPE + MLA_ROPE + half:] = scale
    t[ROW_MC_K, :MLA_ROPE] = 1.0
    t[ROW_MS_K, MLA_ROPE:MLA_ROPE + half] = -1.0
    t[ROW_MS_K, MLA_ROPE + half:2 * MLA_ROPE] = 1.0
    hp = ROPE_PART // 2
    for off in (0, DIL_HEAD_DIM):
        t[ROW_M2_D, off:off + hp] = -1.0
        t[ROW_M1_D, off + hp:off + ROPE_PART] = 1.0
    return jnp.asarray(t)


def _layout_weights(w_in, w_uq, w_ukv):
    o_ckv = MLA_Q_LORA
    o_kr = o_ckv + MLA_KV_LORA
    o_qkv = o_kr + MLA_ROPE
    o_ga = o_qkv + DIL_QKV
    half = MLA_ROPE // 2
    k_raw = w_in[:, o_kr:o_qkv]
    k_swap = jnp.concatenate([k_raw[:, half:], k_raw[:, :half]], axis=1)
    pad = jnp.zeros((D_MODEL, 2 * MLA_Q_LORA - MLA_Q_LORA - MLA_KV_LORA - 2 * MLA_ROPE), w_in.dtype)
    w_in2 = jnp.concatenate(
        [w_in[:, :o_kr], k_raw, k_swap, pad, w_in[:, o_qkv:o_ga]], axis=1).astype(BF16)
    w_gates = w_in[:, o_ga:].astype(BF16)

    wq = w_uq.reshape(MLA_Q_LORA, MLA_HEADS, MLA_NOPE + MLA_ROPE)
    nope, t1, t2 = wq[..., :MLA_NOPE], wq[..., MLA_NOPE:MLA_NOPE + half], wq[..., MLA_NOPE + half:]
    w_uq2 = jnp.concatenate([nope, t1, t2, t2, t1], axis=-1).reshape(
        MLA_Q_LORA, MLA_HEADS * MLA_PAD).astype(BF16)

    wkv = w_ukv.reshape(MLA_KV_LORA, MLA_HEADS, MLA_NOPE + MLA_V)
    k_cols = jnp.concatenate(
        [wkv[..., :MLA_NOPE], jnp.zeros((MLA_KV_LORA, MLA_HEADS, MLA_PAD - MLA_NOPE), w_ukv.dtype)],
        axis=-1).reshape(MLA_KV_LORA, MLA_HEADS * MLA_PAD)
    v_cols = wkv[..., MLA_NOPE:].reshape(MLA_KV_LORA, MLA_HEADS * MLA_V)
    place = np.zeros((MLA_ROPE, MLA_HEADS, MLA_PAD), np.float32)
    for i in range(MLA_ROPE):
        place[i, :, MLA_NOPE + i] = 1.0
        place[i, :, MLA_NOPE + MLA_ROPE + i] = 1.0
    place = jnp.asarray(place.reshape(MLA_ROPE, MLA_HEADS * MLA_PAD))
    n_lhs = 2 * MLA_Q_LORA - MLA_Q_LORA
    k_rows = jnp.concatenate(
        [k_cols, place, place,
         jnp.zeros((n_lhs - MLA_KV_LORA - 2 * MLA_ROPE, MLA_HEADS * MLA_PAD), F32)], axis=0)
    v_rows = jnp.concatenate(
        [v_cols, jnp.zeros((n_lhs - MLA_KV_LORA, MLA_HEADS * MLA_V), F32)], axis=0)
    w_kv = jnp.concatenate([k_rows, v_rows], axis=1).astype(BF16)
    return w_in2, w_gates, w_uq2, w_kv


def _ff_weights(w_gate, w_up, w_down):
    return w_gate.astype(BF16), w_up.astype(BF16), w_down.astype(BF16)


def kernel(x, c, positions, w_ada, b_ada, g_pre_ff1, w_gate1, w_up1, w_down1, g_post_ff1, g_pre_mix, w_in, g_cq, w_uq, g_ckv, w_ukv, w_o_mla, w_o_dil, w_out, g_post_mix, g_pre_ff2, w_gate2, w_up2, w_down2, g_post_ff2):
    bsz, s, d = x.shape
    depth = w_ada.shape[0]
    pos = positions.astype(F32).reshape(bsz, s, 1)
    tables = _tables()
    for l in range(depth):
        ada = _ada(c, w_ada, b_ada, l).reshape(bsz, 3, 3, 1, d)
        mod = lambda i, j: ada[:, i, j]

        x = _ffn(x, mod(0, 0), mod(0, 1), mod(0, 2), g_pre_ff1[l][None], g_post_ff1[l][None],
                 *_ff_weights(w_gate1[l], w_up1[l], w_down1[l]))

        w_in2, w_gates, w_uq2, w_kv = _layout_weights(w_in[l], w_uq[l], w_ukv[l])
        q_m, k_m, v_m, *dil_qkv = _proj(x, pos, mod(1, 0), mod(1, 1), g_pre_mix[l][None], tables,
                                        w_in2, g_cq[l][None], w_uq2, g_ckv[l][None], w_kv)
        o_mla = _mla(q_m, k_m, v_m)

        outs, lses = [], []
        for qkv_g, (window, dil) in zip(dil_qkv, DIL_PATTERNS):
            chunk = min(s // dil, DIL_STEP_QUERIES)
            o_g, lse_g = _dilated(qkv_g, n_side=window // (2 * dil),
                                  n_res=min(dil, DIL_STEP_QUERIES // chunk), chunk=chunk)
            outs.append(o_g)
            lses.append(lse_g)

        x = _post(x, mod(1, 0), mod(1, 1), mod(1, 2), g_pre_mix[l][None], g_post_mix[l][None],
                  w_gates, o_mla, outs, lses,
                  w_o_mla[l].astype(BF16), w_o_dil[l].astype(BF16), w_out[l].astype(BF16))

        x = _ffn(x, mod(2, 0), mod(2, 1), mod(2, 2), g_pre_ff2[l][None], g_post_ff2[l][None],
                 *_ff_weights(w_gate2[l], w_up2[l], w_down2[l]))
    return x
```

```python
import functools

import numpy as np
import jax
import jax.numpy as jnp
from jax import lax
from jax.experimental import pallas as pl
from jax.experimental.pallas import tpu as pltpu

F32 = jnp.float32
BF16 = jnp.bfloat16

D_MODEL = 1024
D_FF = 2816
FFN_RES = 0.5
MLA_HEADS = 8
MLA_Q_LORA = 256
MLA_KV_LORA = 128
MLA_NOPE = 64
MLA_ROPE = 32
MLA_V = 64
MLA_THETA = 10000.0
DIL_PATTERNS = ((128, 1), (512, 4), (2048, 16))
N_DIL_GROUPS = 3
DIL_HEADS = 4
DIL_HEAD_DIM = 64
DIL_GROUP_W = DIL_HEADS * DIL_HEAD_DIM
DIL_QKV = 3 * N_DIL_GROUPS * DIL_GROUP_W
ROPE_THETA = 500000.0
ROPE_PART = DIL_HEAD_DIM // 4
NORM_EPS = 1e-6
NEG_INF = -1e30

LANES = 128
MXU_N = 256
FF_CHUNK = MXU_N
N_FF_CHUNKS = D_FF // FF_CHUNK
MLA_PAD = LANES
MLA_PAIRS = MLA_HEADS * MLA_V // LANES
LOG2E = float(np.log2(np.e))
LN2 = float(np.log(2.0))
FFN_ROW_GROUPS = 2
POST_ROW_GROUPS = 2
PROJ_ROW_GROUPS = 1
VMEM_LIMIT = 56 * 1024 * 1024

ROW_INVF, ROW_ONE_Q, ROW_MC_Q, ROW_MS_Q, ROW_MC_K, ROW_MS_K, ROW_M1_D, ROW_M2_D = range(8)
N_TABLE_ROWS = 8


def _rms(x, g):
    return x * lax.rsqrt(jnp.mean(x * x, axis=-1, keepdims=True) + NORM_EPS) * g


def _silu(x):
    return x * jax.nn.sigmoid(x)


def _dot(a, b):
    return jnp.dot(a, b, preferred_element_type=F32)


def _dot_nt(a, b):
    return lax.dot_general(a, b, (((1,), (1,)), ((), ())), preferred_element_type=F32)


def _const_spec(shape):
    nd = len(shape)
    return pl.BlockSpec(shape, lambda *_: (0,) * nd, pipeline_mode=pl.Buffered(1))


def _ada_kernel(c_ref, w_ref, b_ref, o_ref):
    cond = _silu(c_ref[...])
    o_ref[...] = _dot(cond, w_ref[...]) + b_ref[...]


def _ada(c, w, b, layer):
    bsz, d = c.shape
    depth, _, n = w.shape
    tn = 1024
    return pl.pallas_call(
        _ada_kernel,
        grid=(n // tn,),
        in_specs=[
            pl.BlockSpec((bsz, d), lambda j: (0, 0)),
            pl.BlockSpec((None, d, tn), lambda j: (layer, 0, j)),
            pl.BlockSpec((None, 1, tn), lambda j: (layer, 0, j)),
        ],
        out_specs=pl.BlockSpec((bsz, tn), lambda j: (0, j)),
        out_shape=jax.ShapeDtypeStruct((bsz, n), F32),
        compiler_params=pltpu.CompilerParams(dimension_semantics=("arbitrary",)),
        name="ada",
    )(c, w, b.reshape(depth, 1, n))


def _ffn_kernel(x_ref, shift_ref, scale_ref, gate_ref, gpre_ref, gpost_ref,
                wg_ref, wu_ref, wd_ref, o_ref):
    rows_per = x_ref.shape[0] // FFN_ROW_GROUPS
    parts = [slice(part * rows_per, (part + 1) * rows_per) for part in range(FFN_ROW_GROUPS)]
    hbs = [(_rms(x_ref[rows, :], gpre_ref[...]) * (1.0 + scale_ref[...])
            + shift_ref[...]).astype(BF16) for rows in parts]
    accs = [None] * FFN_ROW_GROUPS

    def gate_up(part, c):
        cols = slice(c * FF_CHUNK, (c + 1) * FF_CHUNK)
        return part, c, _dot(hbs[part], wg_ref[:, cols]), _dot(hbs[part], wu_ref[:, cols])

    def down(part, c, g, u):
        a = (_silu(g) * u).astype(BF16)
        d = _dot(a, wd_ref[c * FF_CHUNK:(c + 1) * FF_CHUNK, :])
        accs[part] = d if accs[part] is None else accs[part] + d
        if c == N_FF_CHUNKS - 1:
            rows = parts[part]
            y = _rms(accs[part], gpost_ref[...])
            o_ref[rows, :] = x_ref[rows, :] + (FFN_RES * gate_ref[...]) * y

    items = [(part, c) for part in range(FFN_ROW_GROUPS) for c in range(N_FF_CHUNKS)]
    pending = gate_up(*items[0])
    for nxt in items[1:]:
        upcoming = gate_up(*nxt)
        down(*pending)
        pending = upcoming
    down(*pending)


def _ffn(x, shift, scale, gate, g_pre, g_post, wg, wu, wd, tm=1024):
    bsz, s, d = x.shape
    tok = pl.BlockSpec((None, tm, d), lambda b, i: (b, i, 0))
    mod = pl.BlockSpec((None, 1, d), lambda b, i: (b, 0, 0))
    return pl.pallas_call(
        _ffn_kernel,
        grid=(bsz, s // tm),
        in_specs=[tok, mod, mod, mod, _const_spec((1, d)), _const_spec((1, d)),
                  _const_spec(wg.shape), _const_spec(wu.shape), _const_spec(wd.shape)],
        out_specs=tok,
        out_shape=jax.ShapeDtypeStruct(x.shape, F32),
        compiler_params=pltpu.CompilerParams(
            dimension_semantics=("arbitrary", "arbitrary"), vmem_limit_bytes=VMEM_LIMIT),
        name="ffn",
    )(x, shift, scale, gate, g_pre, g_post, wg, wu, wd)


def _proj_kernel(x_ref, pos_ref, shift_ref, scale_ref, gpre_ref, tab_ref, win_ref,
                 gcq_ref, wuq_ref, gckv_ref, wkv_ref,
                 q_ref, k_ref, v_ref, d0_ref, d1_ref, d2_ref, split_scr):
    d_refs = (d0_ref, d1_ref, d2_ref)
    rows_per = x_ref.shape[0] // PROJ_ROW_GROUPS
    q_scale = DIL_HEAD_DIM ** -0.5 * LOG2E
    base = 2 * MLA_Q_LORA
    n_chunks = DIL_QKV // MXU_N
    rope_chunks = 2 * N_DIL_GROUPS * DIL_GROUP_W // MXU_N
    q_chunks = N_DIL_GROUPS * DIL_GROUP_W // MXU_N
    halves = MXU_N // LANES
    low_half = lax.broadcasted_iota(jnp.int32, (1, LANES), 1) < LANES // 2

    def row(r):
        return tab_ref[r:r + 1, :]

    def part_stages(part, u):
        rows = slice(part * rows_per, (part + 1) * rows_per)
        ang = pos_ref[rows, :] * row(ROW_INVF)
        cos_a, sin_a = jnp.cos(ang), jnp.sin(ang)
        cos_r, sin_r = pltpu.roll(cos_a, LANES // 2, 1), pltpu.roll(sin_a, LANES // 2, 1)
        held = {}

        def post_cq(c_q):
            held["cq"] = _rms(c_q, gcq_ref[...]).astype(BF16)

        def post_q(q_pad):
            t_q = row(ROW_ONE_Q) + cos_a * row(ROW_MC_Q) + sin_a * row(ROW_MS_Q)
            for h in range(MLA_HEADS):
                q_ref[h, rows, :] = (q_pad[:, h * MLA_PAD:(h + 1) * MLA_PAD] * t_q).astype(BF16)

        def post_ckv(pb):
            c_kv = pb[:, :MLA_KV_LORA]
            r_kv = lax.rsqrt(jnp.mean(c_kv * c_kv, axis=-1, keepdims=True) + NORM_EPS)
            t_k = cos_r * row(ROW_MC_K) + sin_r * row(ROW_MS_K)
            held["kv_lhs"] = jnp.concatenate(
                [c_kv * r_kv * gckv_ref[...], pb[:, MLA_KV_LORA:] * t_k], axis=-1).astype(BF16)

        def post_kv(kv):
            for h in range(MLA_HEADS):
                k_ref[h, rows, :] = kv[:, h * MLA_PAD:(h + 1) * MLA_PAD].astype(BF16)
            v_off = MLA_HEADS * MLA_PAD
            for p in range(MLA_PAIRS):
                v_ref[p, rows, :] = kv[:, v_off + p * LANES: v_off + (p + 1) * LANES].astype(BF16)

        cd = jnp.where(low_half, cos_a, cos_r)
        sd = jnp.where(low_half, sin_a, sin_r)
        s1 = sd * row(ROW_M1_D)
        s2 = sd * row(ROW_M2_D)

        def post_dil(j, pc):
            which, g = divmod(j, N_DIL_GROUPS)
            dil = DIL_PATTERNS[g][1]
            d_ref = d_refs[g]
            n = rows_per // dil
            out_rows = slice(part * n, (part + 1) * n)
            for half in range(halves):
                xs = pc[:, half * LANES:(half + 1) * LANES]
                if j < rope_chunks:
                    xs = (xs * cd + pltpu.roll(xs, ROPE_PART // 2, 1) * s1
                          + pltpu.roll(xs, LANES - ROPE_PART // 2, 1) * s2)
                    if j < q_chunks:
                        xs = xs * q_scale
                lo = which * DIL_GROUP_W + half * LANES
                if dil == 1:
                    d_ref[0, out_rows, lo:lo + LANES] = xs.astype(BF16)
                else:
                    buf = (part * 2 + j % 2) * halves + half
                    split_scr[buf] = xs
                    for r in range(dil):
                        d_ref[r, out_rows, lo:lo + LANES] = (
                            split_scr[buf, pl.ds(r, n, stride=dil), :].astype(BF16))

        def w_in_cols(lo):
            return lambda: _dot(u, win_ref[:, lo:lo + MXU_N])

        return [
            (w_in_cols(0), post_cq),
            (w_in_cols(MLA_Q_LORA), post_ckv),
            (w_in_cols(base), functools.partial(post_dil, 0)),
            (lambda: _dot(held["cq"], wuq_ref[...]), post_q),
            (lambda: _dot(held["kv_lhs"], wkv_ref[...]), post_kv),
        ] + [(w_in_cols(base + j * MXU_N), functools.partial(post_dil, j))
             for j in range(1, n_chunks)]

    us = [(_rms(x_ref[part * rows_per:(part + 1) * rows_per, :], gpre_ref[...])
           * (1.0 + scale_ref[...]) + shift_ref[...]).astype(BF16)
          for part in range(PROJ_ROW_GROUPS)]
    stages = [st for part, u in enumerate(us) for st in part_stages(part, u)]
    result = stages[0][0]()
    for i, (_, post) in enumerate(stages):
        upcoming = stages[i + 1][0]() if i + 1 < len(stages) else None
        post(result)
        result = upcoming


def _proj(x, pos, shift, scale, g_pre, tables, w_in2, g_cq, w_uq2, g_ckv, w_kv, tm=512):
    bsz, s, d = x.shape
    tok = lambda n: pl.BlockSpec((None, tm, n), lambda b, i: (b, i, 0))
    mod = pl.BlockSpec((None, 1, d), lambda b, i: (b, 0, 0))
    heads = lambda n: pl.BlockSpec((None, n, tm, LANES), lambda b, i: (b, 0, i, 0))
    out_shapes = (
        jax.ShapeDtypeStruct((bsz, MLA_HEADS, s, MLA_PAD), BF16),
        jax.ShapeDtypeStruct((bsz, MLA_HEADS, s, MLA_PAD), BF16),
        jax.ShapeDtypeStruct((bsz, MLA_PAIRS, s, LANES), BF16),
    ) + tuple(jax.ShapeDtypeStruct((bsz, dil, s // dil, 3 * DIL_GROUP_W), BF16)
              for _, dil in DIL_PATTERNS)
    split = lambda dil: pl.BlockSpec((None, dil, tm // dil, 3 * DIL_GROUP_W),
                                     lambda b, i: (b, 0, i, 0))
    return pl.pallas_call(
        _proj_kernel,
        grid=(bsz, s // tm),
        in_specs=[tok(d), tok(1), mod, mod, _const_spec((1, d)), _const_spec(tables.shape),
                  _const_spec(w_in2.shape), _const_spec(g_cq.shape), _const_spec(w_uq2.shape),
                  _const_spec(g_ckv.shape), _const_spec(w_kv.shape)],
        out_specs=(heads(MLA_HEADS), heads(MLA_HEADS), heads(MLA_PAIRS))
        + tuple(split(dil) for _, dil in DIL_PATTERNS),
        out_shape=out_shapes,
        scratch_shapes=[pltpu.VMEM((PROJ_ROW_GROUPS * 2 * MXU_N // LANES,
                                    tm // PROJ_ROW_GROUPS, LANES), F32)],
        compiler_params=pltpu.CompilerParams(
            dimension_semantics=("arbitrary", "arbitrary"), vmem_limit_bytes=VMEM_LIMIT),
        name="proj",
    )(x, pos, shift, scale, g_pre, tables, w_in2, g_cq, w_uq2, g_ckv, w_kv)


def _attn_kernel(q_ref, k_ref, v_ref, *refs, tk, dil_cfg):
    n_g = len(dil_cfg)
    dil_in, o_ref = refs[:3 * n_g], refs[3 * n_g]
    dil_out = refs[3 * n_g + 1:5 * n_g + 1]
    s_even, s_odd, p_even, p_odd, acc_scr = refs[5 * n_g + 1:]
    tq = q_ref.shape[1]
    n_kv = k_ref.shape[1] // tk
    n_steps = MLA_HEADS * n_kv
    lane = lax.broadcasted_iota(jnp.int32, (1, LANES), 1)
    ones = jnp.ones((tk, LANES), BF16)
    s_buf, p_buf = (s_even, s_odd), (p_even, p_odd)

    def scores(t):
        h, j = divmod(t, n_kv)
        s_buf[t % 2][...] = _dot_nt(q_ref[h], k_ref[h, j * tk:(j + 1) * tk, :])

    def softmax(t, m):
        s = s_buf[t % 2][...]
        tile_max = jnp.max(s, axis=-1, keepdims=True)
        first = t % n_kv == 0
        m_new = tile_max if first else jnp.maximum(m, tile_max)
        p_buf[t % 2][...] = jnp.exp2(s - m_new).astype(BF16)
        return m_new, None if first else jnp.exp2(m - m_new)

    def values(t, alpha):
        h, j = divmod(t, n_kv)
        rhs = jnp.concatenate([v_ref[h // 2, j * tk:(j + 1) * tk, :], ones], axis=1)
        pv = _dot(p_buf[t % 2][...], rhs)
        acc_scr[h % 2] = pv if alpha is None else alpha * acc_scr[h % 2] + pv
        if j == n_kv - 1 and h % 2 == 1:
            halves = [acc_scr[i][:, :LANES] / acc_scr[i][:, LANES:] for i in range(2)]
            o_ref[h // 2] = jnp.where(lane < MLA_V, halves[0], halves[1]).astype(BF16)

    dil_tiles = []
    for g, (n_side, sub, chunks_per_seq) in enumerate(dil_cfg):
        dq, dk, dv = dil_in[3 * g:3 * g + 3]
        chunk = dq.shape[1]
        base = 0 if chunks_per_seq == 1 else (pl.program_id(1) % chunks_per_seq) * chunk
        dil_tiles += _dil_tiles(dq, dk, dv, dil_out[2 * g], dil_out[2 * g + 1],
                                n_side=n_side, sub=sub, base=base)
    dil_every = max(1, (n_steps - 1) // len(dil_tiles))
    dil_next = 1
    dil_pending = (dil_tiles[0][1], dil_tiles[0][0]())

    scores(0)
    scores(1)
    m, alpha = softmax(0, None)
    for t in range(1, n_steps):
        if t + 1 < n_steps:
            scores(t + 1)
        m, alpha_t = softmax(t, m)
        values(t - 1, alpha)
        alpha = alpha_t
        if t % dil_every == 0 and dil_pending is not None:
            upcoming = None
            if dil_next < len(dil_tiles):
                upcoming = (dil_tiles[dil_next][1], dil_tiles[dil_next][0]())
                dil_next += 1
            dil_pending[0](*dil_pending[1])
            dil_pending = upcoming
    values(n_steps - 1, alpha)
    while dil_pending is not None:
        upcoming = None
        if dil_next < len(dil_tiles):
            upcoming = (dil_tiles[dil_next][1], dil_tiles[dil_next][0]())
            dil_next += 1
        dil_pending[0](*dil_pending[1])
        dil_pending = upcoming


def _attention(q, k, v, dil_qkv, tq=512, tk=1024):
    bsz, nh, s, w = q.shape
    gw = DIL_GROUP_W
    in_specs = [
        pl.BlockSpec((None, nh, tq, w), lambda b, i: (b, 0, i, 0)),
        pl.BlockSpec((None, nh, s, w), lambda b, i: (b, 0, 0, 0), pipeline_mode=pl.Buffered(1)),
        pl.BlockSpec((None, v.shape[1], s, w), lambda b, i: (b, 0, 0, 0)),
    ]
    out_specs = [pl.BlockSpec((None, v.shape[1], tq, w), lambda b, i: (b, 0, i, 0))]
    out_shape = [jax.ShapeDtypeStruct(v.shape, BF16)]
    dil_cfg = []
    for qkv_g, (window, dil) in zip(dil_qkv, DIL_PATTERNS):
        seq = s // dil
        n_side = window // (2 * dil)
        chunk = min(seq, tq)
        n_res = tq // chunk
        per_seq = seq // chunk
        sub = seq if seq <= 4 * n_side else 2 * n_side
        dil_cfg.append((n_side, sub, per_seq))

        def qmap(b, i, col=0, per_seq=per_seq):
            return b, i // per_seq, i % per_seq, col

        def kvmap(b, i, col, per_seq=per_seq):
            return b, i // per_seq, 0, col

        qspec = pl.BlockSpec((None, n_res, chunk, gw), qmap)
        in_specs += [qspec,
                     pl.BlockSpec((None, n_res, seq, gw), functools.partial(kvmap, col=1)),
                     pl.BlockSpec((None, n_res, seq, gw), functools.partial(kvmap, col=2))]
        out_specs += [qspec, qspec]
        out_shape += [jax.ShapeDtypeStruct((bsz, dil, seq, gw), BF16),
                      jax.ShapeDtypeStruct((bsz, dil, seq, gw), F32)]
    outs = pl.pallas_call(
        functools.partial(_attn_kernel, tk=tk, dil_cfg=tuple(dil_cfg)),
        grid=(bsz, s // tq),
        in_specs=in_specs,
        out_specs=out_specs,
        out_shape=out_shape,
        scratch_shapes=[pltpu.VMEM((tq, tk), F32), pltpu.VMEM((tq, tk), F32),
                        pltpu.VMEM((tq, tk), BF16), pltpu.VMEM((tq, tk), BF16),
                        pltpu.VMEM((2, tq, 2 * LANES), F32)],
        compiler_params=pltpu.CompilerParams(
            dimension_semantics=("arbitrary", "arbitrary"), vmem_limit_bytes=VMEM_LIMIT),
        name="attention",
    )(q, k, v, *[a for qkv_g in dil_qkv for a in (qkv_g, qkv_g, qkv_g)])
    return outs[0], outs[1::2], outs[2::2]


def _dil_tiles(q_ref, k_ref, v_ref, o_ref, lse_ref, *, n_side, sub, base):
    n_res, chunk, _ = q_ref.shape
    seq = k_ref.shape[1]
    win = min(seq, sub + 2 * n_side)
    rel = (lax.broadcasted_iota(jnp.int32, (sub, win), 1)
           - lax.broadcasted_iota(jnp.int32, (sub, win), 0))
    lane_head = lax.broadcasted_iota(jnp.int32, (1, DIL_GROUP_W), 1) // DIL_HEAD_DIM
    head_masks = [lane_head == h for h in range(DIL_HEADS)]

    def scores(r, t):
        qs = base + t * sub
        if isinstance(qs, int):
            ks = min(max(qs - n_side, 0), seq - win)
        else:
            ks = pl.multiple_of(jnp.clip(qs - n_side, 0, seq - win), n_side)
        q = q_ref[r, t * sub:(t + 1) * sub, :]
        kw = k_ref[r, pl.ds(ks, win), :]
        valid = jnp.abs(rel + (ks - qs)) <= n_side
        s = [jnp.where(valid, _dot_nt(jnp.where(hm, q, jnp.zeros_like(q)), kw), NEG_INF)
             for hm in head_masks]
        return r, t, ks, s

    def finish(r, t, ks, s):
        vw = v_ref[r, pl.ds(ks, win), :]
        m = [jnp.max(sh, axis=-1, keepdims=True) for sh in s]
        p = [jnp.exp2(sh - mh) for sh, mh in zip(s, m)]
        l = [jnp.sum(ph, axis=-1, keepdims=True) for ph in p]
        pv = [_dot(ph.astype(BF16), vw) for ph in p]
        o = jnp.zeros((sub, DIL_GROUP_W), F32)
        lse = jnp.zeros((sub, DIL_GROUP_W), F32)
        for hm, mh, lh, pvh in zip(head_masks, m, l, pv):
            o = jnp.where(hm, pvh / lh, o)
            lse = jnp.where(hm, mh * LN2 + jnp.log(lh), lse)
        o_ref[r, t * sub:(t + 1) * sub, :] = o.astype(BF16)
        lse_ref[r, t * sub:(t + 1) * sub, :] = lse

    return [(functools.partial(scores, r, t), finish)
            for r in range(n_res) for t in range(chunk // sub)]


def _post_kernel(x_ref, shift_ref, scale_ref, gate_ref, gpre_ref, gpost_ref, wgate_ref,
                 omla_ref, o0_ref, o1_ref, o2_ref, l0_ref, l1_ref, l2_ref,
                 womla_ref, wodil_ref, wout_ref, out_ref, merge_scr):
    tm = x_ref.shape[0]
    rows_per = tm // POST_ROW_GROUPS
    halves = DIL_GROUP_W // LANES

    def natural(ref, part, slot):
        dil = ref.shape[0]
        n = rows_per // dil
        if dil == 1:
            return ref[0, part * n:(part + 1) * n, :].astype(F32)
        for r in range(dil):
            blk = ref[r, part * n:(part + 1) * n, :].astype(F32)
            for hf in range(halves):
                merge_scr[slot * halves + hf, pl.ds(part * rows_per + r, n, stride=dil), :] = (
                    blk[:, hf * LANES:(hf + 1) * LANES])
        rows = slice(part * rows_per, (part + 1) * rows_per)
        return jnp.concatenate([merge_scr[slot * halves + hf, rows, :] for hf in range(halves)],
                               axis=-1)

    def mixed_dilated(part):
        l0, l1, l2 = natural(l0_ref, part, 0), natural(l1_ref, part, 1), natural(l2_ref, part, 2)
        mx = jnp.maximum(jnp.maximum(l0, l1), l2)
        e0, e1, e2 = jnp.exp(l0 - mx), jnp.exp(l1 - mx), jnp.exp(l2 - mx)
        inv = 1.0 / (e0 + e1 + e2)
        return ((e0 * inv) * natural(o0_ref, part, 3) + (e1 * inv) * natural(o1_ref, part, 4)
                + (e2 * inv) * natural(o2_ref, part, 5)).astype(BF16)

    parts = [slice(p * rows_per, (p + 1) * rows_per) for p in range(POST_ROW_GROUPS)]
    us = [(_rms(x_ref[rows, :], gpre_ref[...]) * (1.0 + scale_ref[...])
           + shift_ref[...]).astype(BF16) for rows in parts]

    def first_matmuls(part):
        rows = parts[part]
        o_mla = jnp.concatenate([omla_ref[p, rows, :] for p in range(MLA_PAIRS)], axis=-1)
        return (_dot(us[part], wgate_ref[:, :D_MODEL]), _dot(us[part], wgate_ref[:, D_MODEL:]),
                _dot(o_mla, womla_ref[...]))

    def branch_matmuls(part):
        return first_matmuls(part) + (_dot(mixed_dilated(part), wodil_ref[...]),)

    def merged(ga, gb, ma, md):
        return (jax.nn.sigmoid(ga) * ma + jax.nn.sigmoid(gb) * md).astype(BF16)

    def finish(part, y):
        rows = parts[part]
        out_ref[rows, :] = x_ref[rows, :] + gate_ref[...] * _rms(y, gpost_ref[...])

    branches = [branch_matmuls(part) for part in range(POST_ROW_GROUPS)]
    ys = [_dot(merged(*br), wout_ref[...]) for br in branches]
    for part, y in enumerate(ys):
        finish(part, y)


def _post(x, shift, scale, gate, g_pre, g_post, w_gates, o_mla, o_dil, lse_dil,
          w_o_mla, w_o_dil, w_out, tm=1024):
    bsz, s, d = x.shape
    tok = lambda n: pl.BlockSpec((None, tm, n), lambda b, i: (b, i, 0))
    mod = pl.BlockSpec((None, 1, d), lambda b, i: (b, 0, 0))
    gw = DIL_GROUP_W
    split = [pl.BlockSpec((None, dil, tm // dil, gw), lambda b, i: (b, 0, i, 0))
             for _, dil in DIL_PATTERNS]
    return pl.pallas_call(
        _post_kernel,
        grid=(bsz, s // tm),
        in_specs=[tok(d), mod, mod, mod, _const_spec((1, d)), _const_spec((1, d)),
                  _const_spec(w_gates.shape),
                  pl.BlockSpec((None, MLA_PAIRS, tm, LANES), lambda b, i: (b, 0, i, 0)),
                  *split, *split,
                  _const_spec(w_o_mla.shape), _const_spec(w_o_dil.shape),
                  _const_spec(w_out.shape)],
        out_specs=tok(d),
        out_shape=jax.ShapeDtypeStruct(x.shape, F32),
        scratch_shapes=[pltpu.VMEM((2 * N_DIL_GROUPS * gw // LANES, tm, LANES), F32)],
        compiler_params=pltpu.CompilerParams(
            dimension_semantics=("arbitrary", "arbitrary"), vmem_limit_bytes=VMEM_LIMIT),
        name="post",
    )(x, shift, scale, gate, g_pre, g_post, w_gates, o_mla, *o_dil, *lse_dil,
      w_o_mla, w_o_dil, w_out)


def _tables():
    t = np.zeros((N_TABLE_ROWS, LANES), np.float32)
    scale = np.float32((MLA_NOPE + MLA_ROPE) ** -0.5 * LOG2E)
    half = MLA_ROPE // 2
    f_m = MLA_THETA ** (-np.arange(0, MLA_ROPE, 2, dtype=np.float32) / MLA_ROPE)
    f_d = ROPE_THETA ** (-np.arange(0, ROPE_PART, 2, dtype=np.float32) / ROPE_PART)
    t[ROW_INVF, :ROPE_PART] = np.tile(f_d, 2)
    t[ROW_INVF, LANES // 2:] = np.tile(f_m, 4)
    t[ROW_ONE_Q, :MLA_NOPE] = scale
    t[ROW_MC_Q, MLA_NOPE:MLA_NOPE + MLA_ROPE] = scale
    t[ROW_MS_Q, MLA_NOPE + MLA_ROPE:MLA_NOPE + MLA_ROPE + half] = -scale
    t[ROW_MS_Q, MLA_NOPE + MLA_ROPE + half:] = scale
    t[ROW_MC_K, :MLA_ROPE] = 1.0
    t[ROW_MS_K, MLA_ROPE:MLA_ROPE + half] = -1.0
    t[ROW_MS_K, MLA_ROPE + half:2 * MLA_ROPE] = 1.0
    hp = ROPE_PART // 2
    for off in (0, DIL_HEAD_DIM):
        t[ROW_M2_D, off:off + hp] = -1.0
        t[ROW_M1_D, off + hp:off + ROPE_PART] = 1.0
    return jnp.asarray(t)


def _layout_weights(w_in, w_uq, w_ukv):
    o_ckv = MLA_Q_LORA
    o_kr = o_ckv + MLA_KV_LORA
    o_qkv = o_kr + MLA_ROPE
    o_ga = o_qkv + DIL_QKV
    half = MLA_ROPE // 2
    k_raw = w_in[:, o_kr:o_qkv]
    k_swap = jnp.concatenate([k_raw[:, half:], k_raw[:, :half]], axis=1)
    pad = jnp.zeros((D_MODEL, 2 * MLA_Q_LORA - MLA_Q_LORA - MLA_KV_LORA - 2 * MLA_ROPE), w_in.dtype)
    w_in2 = jnp.concatenate(
        [w_in[:, :o_kr], k_raw, k_swap, pad, w_in[:, o_qkv:o_ga]], axis=1).astype(BF16)
    w_gates = w_in[:, o_ga:].astype(BF16)

    wq = w_uq.reshape(MLA_Q_LORA, MLA_HEADS, MLA_NOPE + MLA_ROPE)
    nope, t1, t2 = wq[..., :MLA_NOPE], wq[..., MLA_NOPE:MLA_NOPE + half], wq[..., MLA_NOPE + half:]
    w_uq2 = jnp.concatenate([nope, t1, t2, t2, t1], axis=-1).reshape(
        MLA_Q_LORA, MLA_HEADS * MLA_PAD).astype(BF16)

    wkv = w_ukv.reshape(MLA_KV_LORA, MLA_HEADS, MLA_NOPE + MLA_V)
    k_cols = jnp.concatenate(
        [wkv[..., :MLA_NOPE], jnp.zeros((MLA_KV_LORA, MLA_HEADS, MLA_PAD - MLA_NOPE), w_ukv.dtype)],
        axis=-1).reshape(MLA_KV_LORA, MLA_HEADS * MLA_PAD)
    v_cols = wkv[..., MLA_NOPE:].reshape(MLA_KV_LORA, MLA_HEADS * MLA_V)
    place = np.zeros((MLA_ROPE, MLA_HEADS, MLA_PAD), np.float32)
    for i in range(MLA_ROPE):
        place[i, :, MLA_NOPE + i] = 1.0
        place[i, :, MLA_NOPE + MLA_ROPE + i] = 1.0
    place = jnp.asarray(place.reshape(MLA_ROPE, MLA_HEADS * MLA_PAD))
    n_lhs = 2 * MLA_Q_LORA - MLA_Q_LORA
    k_rows = jnp.concatenate(
        [k_cols, place, place,
         jnp.zeros((n_lhs - MLA_KV_LORA - 2 * MLA_ROPE, MLA_HEADS * MLA_PAD), F32)], axis=0)
    v_rows = jnp.concatenate(
        [v_cols, jnp.zeros((n_lhs - MLA_KV_LORA, MLA_HEADS * MLA_V), F32)], axis=0)
    w_kv = jnp.concatenate([k_rows, v_rows], axis=1).astype(BF16)
    return w_in2, w_gates, w_uq2, w_kv


def _ff_weights(w_gate, w_up, w_down):
    return w_gate.astype(BF16), w_up.astype(BF16), w_down.astype(BF16)


def kernel(x, c, positions, w_ada, b_ada, g_pre_ff1, w_gate1, w_up1, w_down1, g_post_ff1, g_pre_mix, w_in, g_cq, w_uq, g_ckv, w_ukv, w_o_mla, w_o_dil, w_out, g_post_mix, g_pre_ff2, w_gate2, w_up2, w_down2, g_post_ff2):
    bsz, s, d = x.shape
    depth = w_ada.shape[0]
    pos = positions.astype(F32).reshape(bsz, s, 1)
    tables = _tables()
    for l in range(depth):
        ada = _ada(c, w_ada, b_ada, l).reshape(bsz, 3, 3, 1, d)
        mod = lambda i, j: ada[:, i, j]

        x = _ffn(x, mod(0, 0), mod(0, 1), mod(0, 2), g_pre_ff1[l][None], g_post_ff1[l][None],
                 *_ff_weights(w_gate1[l], w_up1[l], w_down1[l]))

        w_in2, w_gates, w_uq2, w_kv = _layout_weights(w_in[l], w_uq[l], w_ukv[l])
        q_m, k_m, v_m, *dil_qkv = _proj(x, pos, mod(1, 0), mod(1, 1), g_pre_mix[l][None], tables,
                                        w_in2, g_cq[l][None], w_uq2, g_ckv[l][None], w_kv)
        o_mla, outs, lses = _attention(q_m, k_m, v_m, dil_qkv)

        x = _post(x, mod(1, 0), mod(1, 1), mod(1, 2), g_pre_mix[l][None], g_post_mix[l][None],
                  w_gates, o_mla, outs, lses,
                  w_o_mla[l].astype(BF16), w_o_dil[l].astype(BF16), w_out[l].astype(BF16))

        x = _ffn(x, mod(2, 0), mod(2, 1), mod(2, 2), g_pre_ff2[l][None], g_post_ff2[l][None],
                 *_ff_weights(w_gate2[l], w_up2[l], w_down2[l]))
    return x
```

```python
import functools

import numpy as np
import jax
import jax.numpy as jnp
from jax import lax
from jax.experimental import pallas as pl
from jax.experimental.pallas import tpu as pltpu

F32 = jnp.float32
BF16 = jnp.bfloat16

D_MODEL = 1024
D_FF = 2816
FFN_RES = 0.5
MLA_HEADS = 8
MLA_Q_LORA = 256
MLA_KV_LORA = 128
MLA_NOPE = 64
MLA_ROPE = 32
MLA_V = 64
MLA_THETA = 10000.0
DIL_PATTERNS = ((128, 1), (512, 4), (2048, 16))
N_DIL_GROUPS = 3
DIL_HEADS = 4
DIL_HEAD_DIM = 64
DIL_GROUP_W = DIL_HEADS * DIL_HEAD_DIM
DIL_QKV = 3 * N_DIL_GROUPS * DIL_GROUP_W
ROPE_THETA = 500000.0
ROPE_PART = DIL_HEAD_DIM // 4
NORM_EPS = 1e-6
NEG_INF = -1e30

LANES = 128
MXU_N = 256
FF_CHUNK = MXU_N
N_FF_CHUNKS = D_FF // FF_CHUNK
MLA_PAD = LANES
MLA_PAIRS = MLA_HEADS * MLA_V // LANES
LOG2E = float(np.log2(np.e))
LN2 = float(np.log(2.0))
DIL_STEP_QUERIES = 1024
FFN_ROW_GROUPS = 2
POST_ROW_GROUPS = 2
PROJ_ROW_GROUPS = 1
VMEM_LIMIT = 56 * 1024 * 1024

ROW_INVF, ROW_ONE_Q, ROW_MC_Q, ROW_MS_Q, ROW_MC_K, ROW_MS_K, ROW_M1_D, ROW_M2_D = range(8)
N_TABLE_ROWS = 8


def _rms(x, g):
    return x * lax.rsqrt(jnp.mean(x * x, axis=-1, keepdims=True) + NORM_EPS) * g


def _silu(x):
    return x * jax.nn.sigmoid(x)


def _dot(a, b):
    return jnp.dot(a, b, preferred_element_type=F32)


def _dot_nt(a, b):
    return lax.dot_general(a, b, (((1,), (1,)), ((), ())), preferred_element_type=F32)


def _const_spec(shape):
    nd = len(shape)
    return pl.BlockSpec(shape, lambda *_: (0,) * nd, pipeline_mode=pl.Buffered(1))


def _ada_kernel(c_ref, w_ref, b_ref, o_ref):
    cond = _silu(c_ref[...])
    o_ref[...] = _dot(cond, w_ref[...]) + b_ref[...]


def _ada(c, w, b, layer):
    bsz, d = c.shape
    depth, _, n = w.shape
    tn = 1024
    return pl.pallas_call(
        _ada_kernel,
        grid=(n // tn,),
        in_specs=[
            pl.BlockSpec((bsz, d), lambda j: (0, 0)),
            pl.BlockSpec((None, d, tn), lambda j: (layer, 0, j)),
            pl.BlockSpec((None, 1, tn), lambda j: (layer, 0, j)),
        ],
        out_specs=pl.BlockSpec((bsz, tn), lambda j: (0, j)),
        out_shape=jax.ShapeDtypeStruct((bsz, n), F32),
        compiler_params=pltpu.CompilerParams(dimension_semantics=("arbitrary",)),
        name="ada",
    )(c, w, b.reshape(depth, 1, n))


def _ffn_kernel(x_ref, shift_ref, scale_ref, gate_ref, gpre_ref, gpost_ref,
                wg_ref, wu_ref, wd_ref, o_ref):
    rows_per = x_ref.shape[0] // FFN_ROW_GROUPS
    parts = [slice(part * rows_per, (part + 1) * rows_per) for part in range(FFN_ROW_GROUPS)]
    hbs = [(_rms(x_ref[rows, :], gpre_ref[...]) * (1.0 + scale_ref[...])
            + shift_ref[...]).astype(BF16) for rows in parts]
    accs = [None] * FFN_ROW_GROUPS

    def gate_up(part, c):
        cols = slice(c * FF_CHUNK, (c + 1) * FF_CHUNK)
        return part, c, _dot(hbs[part], wg_ref[:, cols]), _dot(hbs[part], wu_ref[:, cols])

    def down(part, c, g, u):
        a = (_silu(g) * u).astype(BF16)
        d = _dot(a, wd_ref[c * FF_CHUNK:(c + 1) * FF_CHUNK, :])
        accs[part] = d if accs[part] is None else accs[part] + d
        if c == N_FF_CHUNKS - 1:
            rows = parts[part]
            y = _rms(accs[part], gpost_ref[...])
            o_ref[rows, :] = x_ref[rows, :] + (FFN_RES * gate_ref[...]) * y

    items = [(part, c) for part in range(FFN_ROW_GROUPS) for c in range(N_FF_CHUNKS)]
    pending = gate_up(*items[0])
    for nxt in items[1:]:
        upcoming = gate_up(*nxt)
        down(*pending)
        pending = upcoming
    down(*pending)


def _ffn(x, shift, scale, gate, g_pre, g_post, wg, wu, wd, tm=1024):
    bsz, s, d = x.shape
    tok = pl.BlockSpec((None, tm, d), lambda b, i: (b, i, 0))
    mod = pl.BlockSpec((None, 1, d), lambda b, i: (b, 0, 0))
    return pl.pallas_call(
        _ffn_kernel,
        grid=(bsz, s // tm),
        in_specs=[tok, mod, mod, mod, _const_spec((1, d)), _const_spec((1, d)),
                  _const_spec(wg.shape), _const_spec(wu.shape), _const_spec(wd.shape)],
        out_specs=tok,
        out_shape=jax.ShapeDtypeStruct(x.shape, F32),
        compiler_params=pltpu.CompilerParams(
            dimension_semantics=("arbitrary", "arbitrary"), vmem_limit_bytes=VMEM_LIMIT),
        name="ffn",
    )(x, shift, scale, gate, g_pre, g_post, wg, wu, wd)


def _proj_kernel(x_ref, pos_ref, shift_ref, scale_ref, gpre_ref, tab_ref, win_ref,
                 gcq_ref, wuq_ref, gckv_ref, wkv_ref,
                 q_ref, k_ref, v_ref, d0_ref, d1_ref, d2_ref, split_scr):
    d_refs = (d0_ref, d1_ref, d2_ref)
    rows_per = x_ref.shape[0] // PROJ_ROW_GROUPS
    q_scale = DIL_HEAD_DIM ** -0.5 * LOG2E
    base = 2 * MLA_Q_LORA
    n_chunks = DIL_QKV // MXU_N
    rope_chunks = 2 * N_DIL_GROUPS * DIL_GROUP_W // MXU_N
    q_chunks = N_DIL_GROUPS * DIL_GROUP_W // MXU_N
    halves = MXU_N // LANES
    low_half = lax.broadcasted_iota(jnp.int32, (1, LANES), 1) < LANES // 2

    def row(r):
        return tab_ref[r:r + 1, :]

    def part_stages(part, u):
        rows = slice(part * rows_per, (part + 1) * rows_per)
        ang = pos_ref[rows, :] * row(ROW_INVF)
        cos_a, sin_a = jnp.cos(ang), jnp.sin(ang)
        cos_r, sin_r = pltpu.roll(cos_a, LANES // 2, 1), pltpu.roll(sin_a, LANES // 2, 1)
        held = {}

        def post_cq(c_q):
            held["cq"] = _rms(c_q, gcq_ref[...]).astype(BF16)

        def post_q(q_pad):
            t_q = row(ROW_ONE_Q) + cos_a * row(ROW_MC_Q) + sin_a * row(ROW_MS_Q)
            for h in range(MLA_HEADS):
                q_ref[h, rows, :] = (q_pad[:, h * MLA_PAD:(h + 1) * MLA_PAD] * t_q).astype(BF16)

        def post_ckv(pb):
            c_kv = pb[:, :MLA_KV_LORA]
            r_kv = lax.rsqrt(jnp.mean(c_kv * c_kv, axis=-1, keepdims=True) + NORM_EPS)
            t_k = cos_r * row(ROW_MC_K) + sin_r * row(ROW_MS_K)
            held["kv_lhs"] = jnp.concatenate(
                [c_kv * r_kv * gckv_ref[...], pb[:, MLA_KV_LORA:] * t_k], axis=-1).astype(BF16)

        def post_kv(kv):
            for h in range(MLA_HEADS):
                k_ref[h, rows, :] = kv[:, h * MLA_PAD:(h + 1) * MLA_PAD].astype(BF16)
            v_off = MLA_HEADS * MLA_PAD
            for p in range(MLA_PAIRS):
                v_ref[p, rows, :] = kv[:, v_off + p * LANES: v_off + (p + 1) * LANES].astype(BF16)

        cd = jnp.where(low_half, cos_a, cos_r)
        sd = jnp.where(low_half, sin_a, sin_r)
        s1 = sd * row(ROW_M1_D)
        s2 = sd * row(ROW_M2_D)

        def post_dil(j, pc):
            which, g = divmod(j, N_DIL_GROUPS)
            dil = DIL_PATTERNS[g][1]
            d_ref = d_refs[g]
            n = rows_per // dil
            out_rows = slice(part * n, (part + 1) * n)
            for half in range(halves):
                xs = pc[:, half * LANES:(half + 1) * LANES]
                if j < rope_chunks:
                    xs = (xs * cd + pltpu.roll(xs, ROPE_PART // 2, 1) * s1
                          + pltpu.roll(xs, LANES - ROPE_PART // 2, 1) * s2)
                    if j < q_chunks:
                        xs = xs * q_scale
                lo = which * DIL_GROUP_W + half * LANES
                if dil == 1:
                    d_ref[0, out_rows, lo:lo + LANES] = xs.astype(BF16)
                else:
                    buf = (part * 2 + j % 2) * halves + half
                    split_scr[buf] = xs
                    for r in range(dil):
                        d_ref[r, out_rows, lo:lo + LANES] = (
                            split_scr[buf, pl.ds(r, n, stride=dil), :].astype(BF16))

        def w_in_cols(lo):
            return lambda: _dot(u, win_ref[:, lo:lo + MXU_N])

        return [
            (w_in_cols(0), post_cq),
            (w_in_cols(MLA_Q_LORA), post_ckv),
            (w_in_cols(base), functools.partial(post_dil, 0)),
            (lambda: _dot(held["cq"], wuq_ref[...]), post_q),
            (lambda: _dot(held["kv_lhs"], wkv_ref[...]), post_kv),
        ] + [(w_in_cols(base + j * MXU_N), functools.partial(post_dil, j))
             for j in range(1, n_chunks)]

    us = [(_rms(x_ref[part * rows_per:(part + 1) * rows_per, :], gpre_ref[...])
           * (1.0 + scale_ref[...]) + shift_ref[...]).astype(BF16)
          for part in range(PROJ_ROW_GROUPS)]
    stages = [st for part, u in enumerate(us) for st in part_stages(part, u)]
    result = stages[0][0]()
    for i, (_, post) in enumerate(stages):
        upcoming = stages[i + 1][0]() if i + 1 < len(stages) else None
        post(result)
        result = upcoming


def _proj(x, pos, shift, scale, g_pre, tables, w_in2, g_cq, w_uq2, g_ckv, w_kv, tm=512):
    bsz, s, d = x.shape
    tok = lambda n: pl.BlockSpec((None, tm, n), lambda b, i: (b, i, 0))
    mod = pl.BlockSpec((None, 1, d), lambda b, i: (b, 0, 0))
    heads = lambda n: pl.BlockSpec((None, n, tm, LANES), lambda b, i: (b, 0, i, 0))
    out_shapes = (
        jax.ShapeDtypeStruct((bsz, MLA_HEADS, s, MLA_PAD), BF16),
        jax.ShapeDtypeStruct((bsz, MLA_HEADS, s, MLA_PAD), BF16),
        jax.ShapeDtypeStruct((bsz, MLA_PAIRS, s, LANES), BF16),
    ) + tuple(jax.ShapeDtypeStruct((bsz, dil, s // dil, 3 * DIL_GROUP_W), BF16)
              for _, dil in DIL_PATTERNS)
    split = lambda dil: pl.BlockSpec((None, dil, tm // dil, 3 * DIL_GROUP_W),
                                     lambda b, i: (b, 0, i, 0))
    return pl.pallas_call(
        _proj_kernel,
        grid=(bsz, s // tm),
        in_specs=[tok(d), tok(1), mod, mod, _const_spec((1, d)), _const_spec(tables.shape),
                  _const_spec(w_in2.shape), _const_spec(g_cq.shape), _const_spec(w_uq2.shape),
                  _const_spec(g_ckv.shape), _const_spec(w_kv.shape)],
        out_specs=(heads(MLA_HEADS), heads(MLA_HEADS), heads(MLA_PAIRS))
        + tuple(split(dil) for _, dil in DIL_PATTERNS),
        out_shape=out_shapes,
        scratch_shapes=[pltpu.VMEM((PROJ_ROW_GROUPS * 2 * MXU_N // LANES,
                                    tm // PROJ_ROW_GROUPS, LANES), F32)],
        compiler_params=pltpu.CompilerParams(
            dimension_semantics=("arbitrary", "arbitrary"), vmem_limit_bytes=VMEM_LIMIT),
        name="proj",
    )(x, pos, shift, scale, g_pre, tables, w_in2, g_cq, w_uq2, g_ckv, w_kv)


def _mla_kernel(q_ref, k_ref, v_ref, o_ref, s_even, s_odd, p_even, p_odd, acc_scr, *,
                tk, unroll):
    n_units, tq, _ = acc_scr.shape
    n_kv = k_ref.shape[1] // tk
    kv_bits = n_kv.bit_length() - 1
    head_bits = MLA_HEADS.bit_length() - 1
    assert n_kv == 1 << kv_bits and MLA_HEADS == 1 << head_bits and unroll % 2 == 0
    n_steps = n_units * n_kv
    lane = lax.broadcasted_iota(jnp.int32, (1, LANES), 1)
    ones = jnp.ones((tk, LANES), BF16)

    def split(t):
        unit = lax.shift_right_logical(t, kv_bits)
        return (unit, lax.shift_right_logical(unit, head_bits),
                lax.bitwise_and(unit, MLA_HEADS - 1), lax.bitwise_and(t, n_kv - 1))

    def scores(t, s_scr):
        _, qt, h, j = split(t)
        rows = pl.ds(pl.multiple_of(j * tk, tk), tk)
        q = q_ref[h, pl.ds(pl.multiple_of(qt * tq, tq), tq), :]
        s_scr[...] = _dot_nt(q, k_ref[h, rows, :])

    def softmax(t, m, s_scr, p_scr):
        j = split(t)[3]
        s = s_scr[...]
        m_prev = jnp.where(j == 0, NEG_INF, m)
        m_new = jnp.maximum(m_prev, jnp.max(s, axis=-1, keepdims=True))
        p_scr[...] = jnp.exp2(s - m_new).astype(BF16)
        return m_new, jnp.exp2(m_prev - m_new)

    def values(t, alpha, p_scr):
        unit, _, h, j = split(t)
        rows = pl.ds(pl.multiple_of(j * tk, tk), tk)
        rhs = jnp.concatenate([v_ref[lax.shift_right_logical(h, 1), rows, :], ones], axis=1)
        acc_scr[unit] = alpha * acc_scr[unit] + _dot(p_scr[...], rhs)

    s_buf, p_buf = (s_even, s_odd), (p_even, p_odd)

    def time_step(t, par, m, alpha_prev, with_scores=True):
        if with_scores:
            scores(t + 1, s_buf[1 - par])
        m, alpha = softmax(t, m, s_buf[par], p_buf[par])
        values(t - 1, alpha_prev, p_buf[1 - par])
        return m, alpha

    def body(u, carry):
        t0 = unroll * u + 1
        for i in range(unroll):
            carry = time_step(t0 + i, (1 + i) % 2, *carry)
        return carry

    @pl.when(jnp.logical_and(pl.program_id(0) == 0, pl.program_id(1) == 0))
    def _():
        acc_scr[...] = jnp.zeros(acc_scr.shape, F32)

    scores(jnp.int32(0), s_even)
    scores(jnp.int32(1), s_odd)
    carry = softmax(jnp.int32(0), jnp.full((tq, 1), NEG_INF, F32), s_even, p_even)
    n_iter = (n_steps - 2) // unroll
    carry = lax.fori_loop(0, n_iter, body, carry)
    for t in range(n_iter * unroll + 1, n_steps):
        carry = time_step(jnp.int32(t), t % 2, *carry, with_scores=t + 1 < n_steps)
    values(jnp.int32(n_steps - 1), carry[1], p_buf[(n_steps - 1) % 2])
    for qt in range(n_units // MLA_HEADS):
        for pair in range(MLA_PAIRS):
            halves = []
            for h in (2 * pair, 2 * pair + 1):
                acc = acc_scr[qt * MLA_HEADS + h]
                halves.append(acc[:, :LANES] / acc[:, LANES:])
            o_ref[pair, qt * tq:(qt + 1) * tq, :] = (
                jnp.where(lane < MLA_V, halves[0], halves[1]).astype(BF16))


def _mla(q, k, v, tq=512, q_tiles=1, tk=2048, unroll=14):
    bsz, nh, s, w = q.shape
    rows = tq * q_tiles
    return pl.pallas_call(
        functools.partial(_mla_kernel, tk=tk, unroll=unroll),
        grid=(bsz, s // rows),
        in_specs=[
            pl.BlockSpec((None, nh, rows, w), lambda b, i: (b, 0, i, 0)),
            pl.BlockSpec((None, nh, s, w), lambda b, i: (b, 0, 0, 0)),
            pl.BlockSpec((None, v.shape[1], s, w), lambda b, i: (b, 0, 0, 0)),
        ],
        out_specs=pl.BlockSpec((None, v.shape[1], rows, w), lambda b, i: (b, 0, i, 0)),
        out_shape=jax.ShapeDtypeStruct(v.shape, BF16),
        scratch_shapes=[pltpu.VMEM((tq, tk), F32), pltpu.VMEM((tq, tk), F32),
                        pltpu.VMEM((tq, tk), BF16), pltpu.VMEM((tq, tk), BF16),
                        pltpu.VMEM((q_tiles * nh, tq, 2 * LANES), F32)],
        compiler_params=pltpu.CompilerParams(
            dimension_semantics=("arbitrary", "arbitrary"), vmem_limit_bytes=VMEM_LIMIT),
        name="mla",
    )(q, k, v)


def _dil_kernel(q_ref, k_ref, v_ref, o_ref, lse_ref, *, n_side, sub):
    n_res, chunk, _ = q_ref.shape
    seq = k_ref.shape[1]
    win = min(seq, sub + 2 * n_side)
    base = 0 if chunk == seq else pl.program_id(2) * chunk
    rel = (lax.broadcasted_iota(jnp.int32, (sub, win), 1)
           - lax.broadcasted_iota(jnp.int32, (sub, win), 0))
    lane_head = lax.broadcasted_iota(jnp.int32, (1, DIL_GROUP_W), 1) // DIL_HEAD_DIM
    head_masks = [lane_head == h for h in range(DIL_HEADS)]

    def scores(r, t):
        qs = base + t * sub
        if isinstance(qs, int):
            ks = min(max(qs - n_side, 0), seq - win)
        else:
            ks = pl.multiple_of(jnp.clip(qs - n_side, 0, seq - win), n_side)
        q = q_ref[r, t * sub:(t + 1) * sub, :]
        kw = k_ref[r, pl.ds(ks, win), :]
        valid = jnp.abs(rel + (ks - qs)) <= n_side
        s = [jnp.where(valid, _dot_nt(jnp.where(hm, q, jnp.zeros_like(q)), kw), NEG_INF)
             for hm in head_masks]
        return r, t, ks, s

    def finish(r, t, ks, s):
        vw = v_ref[r, pl.ds(ks, win), :]
        m = [jnp.max(sh, axis=-1, keepdims=True) for sh in s]
        p = [jnp.exp2(sh - mh) for sh, mh in zip(s, m)]
        l = [jnp.sum(ph, axis=-1, keepdims=True) for ph in p]
        pv = [_dot(ph.astype(BF16), vw) for ph in p]
        o = jnp.zeros((sub, DIL_GROUP_W), F32)
        lse = jnp.zeros((sub, DIL_GROUP_W), F32)
        for hm, mh, lh, pvh in zip(head_masks, m, l, pv):
            o = jnp.where(hm, pvh / lh, o)
            lse = jnp.where(hm, mh * LN2 + jnp.log(lh), lse)
        o_ref[r, t * sub:(t + 1) * sub, :] = o.astype(BF16)
        lse_ref[r, t * sub:(t + 1) * sub, :] = lse

    tiles = [(r, t) for r in range(n_res) for t in range(chunk // sub)]
    pending = scores(*tiles[0])
    for nxt in tiles[1:]:
        upcoming = scores(*nxt)
        finish(*pending)
        pending = upcoming
    finish(*pending)


def _dilated(qkv, n_side, n_res, chunk):
    bsz, dil, seq, _ = qkv.shape
    sub = seq if seq <= 4 * n_side else 2 * n_side
    w = DIL_GROUP_W
    qspec = pl.BlockSpec((None, n_res, chunk, w), lambda b, r, i: (b, r, i, 0))
    kspec = pl.BlockSpec((None, n_res, seq, w), lambda b, r, i: (b, r, 0, 1))
    vspec = pl.BlockSpec((None, n_res, seq, w), lambda b, r, i: (b, r, 0, 2))
    out = (bsz, dil, seq, w)
    return pl.pallas_call(
        functools.partial(_dil_kernel, n_side=n_side, sub=sub),
        grid=(bsz, dil // n_res, seq // chunk),
        in_specs=[qspec, kspec, vspec],
        out_specs=(qspec, qspec),
        out_shape=(jax.ShapeDtypeStruct(out, BF16), jax.ShapeDtypeStruct(out, F32)),
        compiler_params=pltpu.CompilerParams(
            dimension_semantics=("arbitrary",) * 3, vmem_limit_bytes=VMEM_LIMIT),
        name="dilated",
    )(qkv, qkv, qkv)


def _post_kernel(x_ref, shift_ref, scale_ref, gate_ref, gpre_ref, gpost_ref, wgate_ref,
                 omla_ref, o0_ref, o1_ref, o2_ref, l0_ref, l1_ref, l2_ref,
                 womla_ref, wodil_ref, wout_ref, out_ref, merge_scr):
    tm = x_ref.shape[0]
    rows_per = tm // POST_ROW_GROUPS
    halves = DIL_GROUP_W // LANES

    def natural(ref, part, slot):
        dil = ref.shape[0]
        n = rows_per // dil
        if dil == 1:
            return ref[0, part * n:(part + 1) * n, :].astype(F32)
        for r in range(dil):
            blk = ref[r, part * n:(part + 1) * n, :].astype(F32)
            for hf in range(halves):
                merge_scr[slot * halves + hf, pl.ds(part * rows_per + r, n, stride=dil), :] = (
                    blk[:, hf * LANES:(hf + 1) * LANES])
        rows = slice(part * rows_per, (part + 1) * rows_per)
        return jnp.concatenate([merge_scr[slot * halves + hf, rows, :] for hf in range(halves)],
                               axis=-1)

    def mixed_dilated(part):
        l0, l1, l2 = natural(l0_ref, part, 0), natural(l1_ref, part, 1), natural(l2_ref, part, 2)
        mx = jnp.maximum(jnp.maximum(l0, l1), l2)
        e0, e1, e2 = jnp.exp(l0 - mx), jnp.exp(l1 - mx), jnp.exp(l2 - mx)
        inv = 1.0 / (e0 + e1 + e2)
        return ((e0 * inv) * natural(o0_ref, part, 3) + (e1 * inv) * natural(o1_ref, part, 4)
                + (e2 * inv) * natural(o2_ref, part, 5)).astype(BF16)

    parts = [slice(p * rows_per, (p + 1) * rows_per) for p in range(POST_ROW_GROUPS)]
    us = [(_rms(x_ref[rows, :], gpre_ref[...]) * (1.0 + scale_ref[...])
           + shift_ref[...]).astype(BF16) for rows in parts]

    def first_matmuls(part):
        rows = parts[part]
        o_mla = jnp.concatenate([omla_ref[p, rows, :] for p in range(MLA_PAIRS)], axis=-1)
        return (_dot(us[part], wgate_ref[:, :D_MODEL]), _dot(us[part], wgate_ref[:, D_MODEL:]),
                _dot(o_mla, womla_ref[...]))

    def merged(part, firsts):
        ga, gb, ma = firsts
        md = _dot(mixed_dilated(part), wodil_ref[...])
        return (jax.nn.sigmoid(ga) * ma + jax.nn.sigmoid(gb) * md).astype(BF16)

    def finish(part, y):
        rows = parts[part]
        out_ref[rows, :] = x_ref[rows, :] + gate_ref[...] * _rms(y, gpost_ref[...])

    firsts = first_matmuls(0)
    for part in range(POST_ROW_GROUPS):
        mg = merged(part, firsts)
        if part + 1 < POST_ROW_GROUPS:
            firsts = first_matmuls(part + 1)
        finish(part, _dot(mg, wout_ref[...]))


def _post(x, shift, scale, gate, g_pre, g_post, w_gates, o_mla, o_dil, lse_dil,
          w_o_mla, w_o_dil, w_out, tm=1024):
    bsz, s, d = x.shape
    tok = lambda n: pl.BlockSpec((None, tm, n), lambda b, i: (b, i, 0))
    mod = pl.BlockSpec((None, 1, d), lambda b, i: (b, 0, 0))
    gw = DIL_GROUP_W
    split = [pl.BlockSpec((None, dil, tm // dil, gw), lambda b, i: (b, 0, i, 0))
             for _, dil in DIL_PATTERNS]
    return pl.pallas_call(
        _post_kernel,
        grid=(bsz, s // tm),
        in_specs=[tok(d), mod, mod, mod, _const_spec((1, d)), _const_spec((1, d)),
                  _const_spec(w_gates.shape),
                  pl.BlockSpec((None, MLA_PAIRS, tm, LANES), lambda b, i: (b, 0, i, 0)),
                  *split, *split,
                  _const_spec(w_o_mla.shape), _const_spec(w_o_dil.shape),
                  _const_spec(w_out.shape)],
        out_specs=tok(d),
        out_shape=jax.ShapeDtypeStruct(x.shape, F32),
        scratch_shapes=[pltpu.VMEM((2 * N_DIL_GROUPS * gw // LANES, tm, LANES), F32)],
        compiler_params=pltpu.CompilerParams(
            dimension_semantics=("arbitrary", "arbitrary"), vmem_limit_bytes=VMEM_LIMIT),
        name="post",
    )(x, shift, scale, gate, g_pre, g_post, w_gates, o_mla, *o_dil, *lse_dil,
      w_o_mla, w_o_dil, w_out)


def _tables():
    t = np.zeros((N_TABLE_ROWS, LANES), np.float32)
    scale = np.float32((MLA_NOPE + MLA_ROPE) ** -0.5 * LOG2E)
    half = MLA_ROPE // 2
    f_m = MLA_THETA ** (-np.arange(0, MLA_ROPE, 2, dtype=np.float32) / MLA_ROPE)
    f_d = ROPE_THETA ** (-np.arange(0, ROPE_PART, 2, dtype=np.float32) / ROPE_PART)
    t[ROW_INVF, :ROPE_PART] = np.tile(f_d, 2)
    t[ROW_INVF, LANES // 2:] = np.tile(f_m, 4)
    t[ROW_ONE_Q, :MLA_NOPE] = scale
    t[ROW_MC_Q, MLA_NOPE:MLA_NOPE + MLA_ROPE] = scale
    t[ROW_MS_Q, MLA_NOPE + MLA_ROPE:MLA_NOPE + MLA_ROPE + half] = -scale
    t[ROW_MS_Q, MLA_NOPE + MLA_ROPE + half:] = scale
    t[ROW_MC_K, :MLA_ROPE] = 1.0
    t[ROW_MS_K, MLA_ROPE:MLA_ROPE + half] = -1.0
    t[ROW_MS_K, MLA_ROPE + half:2 * MLA_ROPE] = 1.0
    hp = ROPE_PART // 2
    for off in (0, DIL_HEAD_DIM):
        t[ROW_M2_D, off:off + hp] = -1.0
        t[ROW_M1_D, off + hp:off + ROPE_PART] = 1.0
    return jnp.asarray(t)


def _layout_weights(w_in, w_uq, w_ukv):
    o_ckv = MLA_Q_LORA
    o_kr = o_ckv + MLA_KV_LORA
    o_qkv = o_kr + MLA_ROPE
    o_ga = o_qkv + DIL_QKV
    half = MLA_ROPE // 2
    k_raw = w_in[:, o_kr:o_qkv]
    k_swap = jnp.concatenate([k_raw[:, half:], k_raw[:, :half]], axis=1)
    pad = jnp.zeros((D_MODEL, 2 * MLA_Q_LORA - MLA_Q_LORA - MLA_KV_LORA - 2 * MLA_ROPE), w_in.dtype)
    w_in2 = jnp.concatenate(
        [w_in[:, :o_kr], k_raw, k_swap, pad, w_in[:, o_qkv:o_ga]], axis=1).astype(BF16)
    w_gates = w_in[:, o_ga:].astype(BF16)

    wq = w_uq.reshape(MLA_Q_LORA, MLA_HEADS, MLA_NOPE + MLA_ROPE)
    nope, t1, t2 = wq[..., :MLA_NOPE], wq[..., MLA_NOPE:MLA_NOPE + half], wq[..., MLA_NOPE + half:]
    w_uq2 = jnp.concatenate([nope, t1, t2, t2, t1], axis=-1).reshape(
        MLA_Q_LORA, MLA_HEADS * MLA_PAD).astype(BF16)

    wkv = w_ukv.reshape(MLA_KV_LORA, MLA_HEADS, MLA_NOPE + MLA_V)
    k_cols = jnp.concatenate(
        [wkv[..., :MLA_NOPE], jnp.zeros((MLA_KV_LORA, MLA_HEADS, MLA_PAD - MLA_NOPE), w_ukv.dtype)],
        axis=-1).reshape(MLA_KV_LORA, MLA_HEADS * MLA_PAD)
    v_cols = wkv[..., MLA_NOPE:].reshape(MLA_KV_LORA, MLA_HEADS * MLA_V)
    place = np.zeros((MLA_ROPE, MLA_HEADS, MLA_PAD), np.float32)
    for i in range(MLA_ROPE):
        place[i, :, MLA_NOPE + i] = 1.0
        place[i, :, MLA_NOPE + MLA_ROPE + i] = 1.0
    place = jnp.asarray(place.reshape(MLA_ROPE, MLA_HEADS * MLA_PAD))
    n_lhs = 2 * MLA_Q_LORA - MLA_Q_LORA
    k_rows = jnp.concatenate(
        [k_cols, place, place,
         jnp.zeros((n_lhs - MLA_KV_LORA - 2 * MLA_ROPE, MLA_HEADS * MLA_PAD), F32)], axis=0)
    v_rows = jnp.concatenate(
        [v_cols, jnp.zeros((n_lhs - MLA_KV_LORA, MLA_HEADS * MLA_V), F32)], axis=0)
    w_kv = jnp.concatenate([k_rows, v_rows], axis=1).astype(BF16)
    return w_in2, w_gates, w_uq2, w_kv


def _ff_weights(w_gate, w_up, w_down):
    return w_gate.astype(BF16), w_up.astype(BF16), w_down.astype(BF16)


def kernel(x, c, positions, w_ada, b_ada, g_pre_ff1, w_gate1, w_up1, w_down1, g_post_ff1, g_pre_mix, w_in, g_cq, w_uq, g_ckv, w_ukv, w_o_mla, w_o_dil, w_out, g_post_mix, g_pre_ff2, w_gate2, w_up2, w_down2, g_post_ff2):
    bsz, s, d = x.shape
    depth = w_ada.shape[0]
    pos = positions.astype(F32).reshape(bsz, s, 1)
    tables = _tables()
    for l in range(depth):
        ada = _ada(c, w_ada, b_ada, l).reshape(bsz, 3, 3, 1, d)
        mod = lambda i, j: ada[:, i, j]

        x = _ffn(x, mod(0, 0), mod(0, 1), mod(0, 2), g_pre_ff1[l][None], g_post_ff1[l][None],
                 *_ff_weights(w_gate1[l], w_up1[l], w_down1[l]))

        w_in2, w_gates, w_uq2, w_kv = _layout_weights(w_in[l], w_uq[l], w_ukv[l])
        q_m, k_m, v_m, *dil_qkv = _proj(x, pos, mod(1, 0), mod(1, 1), g_pre_mix[l][None], tables,
                                        w_in2, g_cq[l][None], w_uq2, g_ckv[l][None], w_kv)
        o_mla = _mla(q_m, k_m, v_m)

        outs, lses = [], []
        for qkv_g, (window, dil) in zip(dil_qkv, DIL_PATTERNS):
            chunk = min(s // dil, DIL_STEP_QUERIES)
            o_g, lse_g = _dilated(qkv_g, n_side=window // (2 * dil),
                                  n_res=min(dil, DIL_STEP_QUERIES // chunk), chunk=chunk)
            outs.append(o_g)
            lses.append(lse_g)

        x = _post(x, mod(1, 0), mod(1, 1), mod(1, 2), g_pre_mix[l][None], g_post_mix[l][None],
                  w_gates, o_mla, outs, lses,
                  w_o_mla[l].astype(BF16), w_o_dil[l].astype(BF16), w_out[l].astype(BF16))

        x = _ffn(x, mod(2, 0), mod(2, 1), mod(2, 2), g_pre_ff2[l][None], g_post_ff2[l][None],
                 *_ff_weights(w_gate2[l], w_up2[l], w_down2[l]))
    return x
```

```python
import functools

import numpy as np
import jax
import jax.numpy as jnp
from jax import lax
from jax.experimental import pallas as pl
from jax.experimental.pallas import tpu as pltpu

F32 = jnp.float32
BF16 = jnp.bfloat16

D_MODEL = 1024
D_FF = 2816
FFN_RES = 0.5
MLA_HEADS = 8
MLA_Q_LORA = 256
MLA_KV_LORA = 128
MLA_NOPE = 64
MLA_ROPE = 32
MLA_V = 64
MLA_THETA = 10000.0
DIL_PATTERNS = ((128, 1), (512, 4), (2048, 16))
N_DIL_GROUPS = 3
DIL_HEADS = 4
DIL_HEAD_DIM = 64
DIL_GROUP_W = DIL_HEADS * DIL_HEAD_DIM
DIL_QKV = 3 * N_DIL_GROUPS * DIL_GROUP_W
ROPE_THETA = 500000.0
ROPE_PART = DIL_HEAD_DIM // 4
NORM_EPS = 1e-6
NEG_INF = -1e30

LANES = 128
MXU_N = 256
FF_CHUNK = MXU_N
N_FF_CHUNKS = D_FF // FF_CHUNK
MLA_PAD = LANES
MLA_PAIRS = MLA_HEADS * MLA_V // LANES
LOG2E = float(np.log2(np.e))
LN2 = float(np.log(2.0))
DIL_STEP_QUERIES = 1024
FFN_ROW_GROUPS = 2
POST_ROW_GROUPS = 2
PROJ_ROW_GROUPS = 1
VMEM_LIMIT = 56 * 1024 * 1024

ROW_INVF, ROW_ONE_Q, ROW_MC_Q, ROW_MS_Q, ROW_MC_K, ROW_MS_K, ROW_M1_D, ROW_M2_D = range(8)
N_TABLE_ROWS = 8


def _rms(x, g):
    return x * lax.rsqrt(jnp.mean(x * x, axis=-1, keepdims=True) + NORM_EPS) * g


def _silu(x):
    return x * jax.nn.sigmoid(x)


def _dot(a, b):
    return jnp.dot(a, b, preferred_element_type=F32)


def _dot_nt(a, b):
    return lax.dot_general(a, b, (((1,), (1,)), ((), ())), preferred_element_type=F32)


def _const_spec(shape):
    nd = len(shape)
    return pl.BlockSpec(shape, lambda *_: (0,) * nd, pipeline_mode=pl.Buffered(1))


def _ada_kernel(c_ref, w_ref, b_ref, o_ref):
    cond = _silu(c_ref[...])
    o_ref[...] = _dot(cond, w_ref[...]) + b_ref[...]


def _ada(c, w, b, layer):
    bsz, d = c.shape
    depth, _, n = w.shape
    tn = 1024
    return pl.pallas_call(
        _ada_kernel,
        grid=(n // tn,),
        in_specs=[
            pl.BlockSpec((bsz, d), lambda j: (0, 0)),
            pl.BlockSpec((None, d, tn), lambda j: (layer, 0, j)),
            pl.BlockSpec((None, 1, tn), lambda j: (layer, 0, j)),
        ],
        out_specs=pl.BlockSpec((bsz, tn), lambda j: (0, j)),
        out_shape=jax.ShapeDtypeStruct((bsz, n), F32),
        compiler_params=pltpu.CompilerParams(dimension_semantics=("arbitrary",)),
        name="ada",
    )(c, w, b.reshape(depth, 1, n))


def _ffn_kernel(x_ref, shift_ref, scale_ref, gate_ref, gpre_ref, gpost_ref,
                wg_ref, wu_ref, wd_ref, o_ref):
    rows_per = x_ref.shape[0] // FFN_ROW_GROUPS
    parts = [slice(part * rows_per, (part + 1) * rows_per) for part in range(FFN_ROW_GROUPS)]
    hbs = [(_rms(x_ref[rows, :], gpre_ref[...]) * (1.0 + scale_ref[...])
            + shift_ref[...]).astype(BF16) for rows in parts]
    accs = [None] * FFN_ROW_GROUPS

    def gate_up(part, c):
        cols = slice(c * FF_CHUNK, (c + 1) * FF_CHUNK)
        return part, c, _dot(hbs[part], wg_ref[:, cols]), _dot(hbs[part], wu_ref[:, cols])

    def down(part, c, g, u):
        a = (_silu(g) * u).astype(BF16)
        d = _dot(a, wd_ref[c * FF_CHUNK:(c + 1) * FF_CHUNK, :])
        accs[part] = d if accs[part] is None else accs[part] + d
        if c == N_FF_CHUNKS - 1:
            rows = parts[part]
            y = _rms(accs[part], gpost_ref[...])
            o_ref[rows, :] = x_ref[rows, :] + (FFN_RES * gate_ref[...]) * y

    items = [(part, c) for part in range(FFN_ROW_GROUPS) for c in range(N_FF_CHUNKS)]
    pending = gate_up(*items[0])
    for nxt in items[1:]:
        upcoming = gate_up(*nxt)
        down(*pending)
        pending = upcoming
    down(*pending)


def _ffn(x, shift, scale, gate, g_pre, g_post, wg, wu, wd, tm=1024):
    bsz, s, d = x.shape
    tok = pl.BlockSpec((None, tm, d), lambda b, i: (b, i, 0))
    mod = pl.BlockSpec((None, 1, d), lambda b, i: (b, 0, 0))
    return pl.pallas_call(
        _ffn_kernel,
        grid=(bsz, s // tm),
        in_specs=[tok, mod, mod, mod, _const_spec((1, d)), _const_spec((1, d)),
                  _const_spec(wg.shape), _const_spec(wu.shape), _const_spec(wd.shape)],
        out_specs=tok,
        out_shape=jax.ShapeDtypeStruct(x.shape, F32),
        compiler_params=pltpu.CompilerParams(
            dimension_semantics=("arbitrary", "arbitrary"), vmem_limit_bytes=VMEM_LIMIT),
        name="ffn",
    )(x, shift, scale, gate, g_pre, g_post, wg, wu, wd)


def _proj_kernel(x_ref, pos_ref, shift_ref, scale_ref, gpre_ref, tab_ref, win_ref,
                 gcq_ref, wuq_ref, gckv_ref, wkv_ref,
                 q_ref, k_ref, v_ref, d0_ref, d1_ref, d2_ref, split_scr):
    d_refs = (d0_ref, d1_ref, d2_ref)
    rows_per = x_ref.shape[0] // PROJ_ROW_GROUPS
    q_scale = DIL_HEAD_DIM ** -0.5 * LOG2E
    base = 2 * MLA_Q_LORA
    n_chunks = DIL_QKV // MXU_N
    rope_chunks = 2 * N_DIL_GROUPS * DIL_GROUP_W // MXU_N
    q_chunks = N_DIL_GROUPS * DIL_GROUP_W // MXU_N
    halves = MXU_N // LANES
    low_half = lax.broadcasted_iota(jnp.int32, (1, LANES), 1) < LANES // 2

    def row(r):
        return tab_ref[r:r + 1, :]

    def part_stages(part, u):
        rows = slice(part * rows_per, (part + 1) * rows_per)
        ang = pos_ref[rows, :] * row(ROW_INVF)
        cos_a, sin_a = jnp.cos(ang), jnp.sin(ang)
        cos_r, sin_r = pltpu.roll(cos_a, LANES // 2, 1), pltpu.roll(sin_a, LANES // 2, 1)
        held = {}

        def post_cq(c_q):
            held["cq"] = _rms(c_q, gcq_ref[...]).astype(BF16)

        def post_q(q_pad):
            t_q = row(ROW_ONE_Q) + cos_a * row(ROW_MC_Q) + sin_a * row(ROW_MS_Q)
            for h in range(MLA_HEADS):
                q_ref[h, rows, :] = (q_pad[:, h * MLA_PAD:(h + 1) * MLA_PAD] * t_q).astype(BF16)

        def post_ckv(pb):
            c_kv = pb[:, :MLA_KV_LORA]
            r_kv = lax.rsqrt(jnp.mean(c_kv * c_kv, axis=-1, keepdims=True) + NORM_EPS)
            t_k = cos_r * row(ROW_MC_K) + sin_r * row(ROW_MS_K)
            held["kv_lhs"] = jnp.concatenate(
                [c_kv * r_kv * gckv_ref[...], pb[:, MLA_KV_LORA:] * t_k], axis=-1).astype(BF16)

        def post_kv(kv):
            for h in range(MLA_HEADS):
                k_ref[h, rows, :] = kv[:, h * MLA_PAD:(h + 1) * MLA_PAD].astype(BF16)
            v_off = MLA_HEADS * MLA_PAD
            for p in range(MLA_PAIRS):
                v_ref[p, rows, :] = kv[:, v_off + p * LANES: v_off + (p + 1) * LANES].astype(BF16)

        cd = jnp.where(low_half, cos_a, cos_r)
        sd = jnp.where(low_half, sin_a, sin_r)
        s1 = sd * row(ROW_M1_D)
        s2 = sd * row(ROW_M2_D)

        def post_dil(j, pc):
            which, g = divmod(j, N_DIL_GROUPS)
            dil = DIL_PATTERNS[g][1]
            d_ref = d_refs[g]
            n = rows_per // dil
            out_rows = slice(part * n, (part + 1) * n)
            for half in range(halves):
                xs = pc[:, half * LANES:(half + 1) * LANES]
                if j < rope_chunks:
                    xs = (xs * cd + pltpu.roll(xs, ROPE_PART // 2, 1) * s1
                          + pltpu.roll(xs, LANES - ROPE_PART // 2, 1) * s2)
                    if j < q_chunks:
                        xs = xs * q_scale
                lo = which * DIL_GROUP_W + half * LANES
                if dil == 1:
                    d_ref[0, out_rows, lo:lo + LANES] = xs.astype(BF16)
                else:
                    buf = (part * 2 + j % 2) * halves + half
                    split_scr[buf] = xs
                    for r in range(dil):
                        d_ref[r, out_rows, lo:lo + LANES] = (
                            split_scr[buf, pl.ds(r, n, stride=dil), :].astype(BF16))

        def w_in_cols(lo):
            return lambda: _dot(u, win_ref[:, lo:lo + MXU_N])

        return [
            (w_in_cols(0), post_cq),
            (w_in_cols(MLA_Q_LORA), post_ckv),
            (w_in_cols(base), functools.partial(post_dil, 0)),
            (lambda: _dot(held["cq"], wuq_ref[...]), post_q),
            (lambda: _dot(held["kv_lhs"], wkv_ref[...]), post_kv),
        ] + [(w_in_cols(base + j * MXU_N), functools.partial(post_dil, j))
             for j in range(1, n_chunks)]

    us = [(_rms(x_ref[part * rows_per:(part + 1) * rows_per, :], gpre_ref[...])
           * (1.0 + scale_ref[...]) + shift_ref[...]).astype(BF16)
          for part in range(PROJ_ROW_GROUPS)]
    stages = [st for part, u in enumerate(us) for st in part_stages(part, u)]
    result = stages[0][0]()
    for i, (_, post) in enumerate(stages):
        upcoming = stages[i + 1][0]() if i + 1 < len(stages) else None
        post(result)
        result = upcoming


def _proj(x, pos, shift, scale, g_pre, tables, w_in2, g_cq, w_uq2, g_ckv, w_kv, tm=512):
    bsz, s, d = x.shape
    tok = lambda n: pl.BlockSpec((None, tm, n), lambda b, i: (b, i, 0))
    mod = pl.BlockSpec((None, 1, d), lambda b, i: (b, 0, 0))
    heads = lambda n: pl.BlockSpec((None, n, tm, LANES), lambda b, i: (b, 0, i, 0))
    out_shapes = (
        jax.ShapeDtypeStruct((bsz, MLA_HEADS, s, MLA_PAD), BF16),
        jax.ShapeDtypeStruct((bsz, MLA_HEADS, s, MLA_PAD), BF16),
        jax.ShapeDtypeStruct((bsz, MLA_PAIRS, s, LANES), BF16),
    ) + tuple(jax.ShapeDtypeStruct((bsz, dil, s // dil, 3 * DIL_GROUP_W), BF16)
              for _, dil in DIL_PATTERNS)
    split = lambda dil: pl.BlockSpec((None, dil, tm // dil, 3 * DIL_GROUP_W),
                                     lambda b, i: (b, 0, i, 0))
    return pl.pallas_call(
        _proj_kernel,
        grid=(bsz, s // tm),
        in_specs=[tok(d), tok(1), mod, mod, _const_spec((1, d)), _const_spec(tables.shape),
                  _const_spec(w_in2.shape), _const_spec(g_cq.shape), _const_spec(w_uq2.shape),
                  _const_spec(g_ckv.shape), _const_spec(w_kv.shape)],
        out_specs=(heads(MLA_HEADS), heads(MLA_HEADS), heads(MLA_PAIRS))
        + tuple(split(dil) for _, dil in DIL_PATTERNS),
        out_shape=out_shapes,
        scratch_shapes=[pltpu.VMEM((PROJ_ROW_GROUPS * 2 * MXU_N // LANES,
                                    tm // PROJ_ROW_GROUPS, LANES), F32)],
        compiler_params=pltpu.CompilerParams(
            dimension_semantics=("arbitrary", "arbitrary"), vmem_limit_bytes=VMEM_LIMIT),
        name="proj",
    )(x, pos, shift, scale, g_pre, tables, w_in2, g_cq, w_uq2, g_ckv, w_kv)


def _mla_kernel(q_ref, k_ref, v_ref, o_ref, s_even, s_odd, p_even, p_odd, acc_scr, *,
                tk, unroll):
    n_units, tq, _ = acc_scr.shape
    n_kv = k_ref.shape[1] // tk
    kv_bits = n_kv.bit_length() - 1
    head_bits = MLA_HEADS.bit_length() - 1
    assert n_kv == 1 << kv_bits and MLA_HEADS == 1 << head_bits and unroll % 2 == 0
    n_steps = n_units * n_kv
    lane = lax.broadcasted_iota(jnp.int32, (1, LANES), 1)
    ones = jnp.ones((tk, LANES), BF16)

    def split(t):
        unit = lax.shift_right_logical(t, kv_bits)
        return (unit, lax.shift_right_logical(unit, head_bits),
                lax.bitwise_and(unit, MLA_HEADS - 1), lax.bitwise_and(t, n_kv - 1))

    def scores(t, s_scr):
        _, qt, h, j = split(t)
        rows = pl.ds(pl.multiple_of(j * tk, tk), tk)
        q = q_ref[h, pl.ds(pl.multiple_of(qt * tq, tq), tq), :]
        s_scr[...] = _dot_nt(q, k_ref[h, rows, :])

    def softmax(t, m, s_scr, p_scr):
        j = split(t)[3]
        s = s_scr[...]
        m_prev = jnp.where(j == 0, NEG_INF, m)
        m_new = jnp.maximum(m_prev, jnp.max(s, axis=-1, keepdims=True))
        p_scr[...] = jnp.exp2(s - m_new).astype(BF16)
        return m_new, jnp.exp2(m_prev - m_new)

    def values(t, alpha, p_scr):
        unit, _, h, j = split(t)
        rows = pl.ds(pl.multiple_of(j * tk, tk), tk)
        rhs = jnp.concatenate([v_ref[lax.shift_right_logical(h, 1), rows, :], ones], axis=1)
        acc_scr[unit] = alpha * acc_scr[unit] + _dot(p_scr[...], rhs)

    s_buf, p_buf = (s_even, s_odd), (p_even, p_odd)

    def time_step(t, par, m, alpha_prev, with_scores=True):
        if with_scores:
            scores(t + 1, s_buf[1 - par])
        m, alpha = softmax(t, m, s_buf[par], p_buf[par])
        values(t - 1, alpha_prev, p_buf[1 - par])
        return m, alpha

    def body(u, carry):
        t0 = unroll * u + 1
        for i in range(unroll):
            carry = time_step(t0 + i, (1 + i) % 2, *carry)
        return carry

    @pl.when(jnp.logical_and(pl.program_id(0) == 0, pl.program_id(1) == 0))
    def _():
        acc_scr[...] = jnp.zeros(acc_scr.shape, F32)

    scores(jnp.int32(0), s_even)
    scores(jnp.int32(1), s_odd)
    carry = softmax(jnp.int32(0), jnp.full((tq, 1), NEG_INF, F32), s_even, p_even)
    n_iter = (n_steps - 2) // unroll
    carry = lax.fori_loop(0, n_iter, body, carry)
    for t in range(n_iter * unroll + 1, n_steps):
        carry = time_step(jnp.int32(t), t % 2, *carry, with_scores=t + 1 < n_steps)
    values(jnp.int32(n_steps - 1), carry[1], p_buf[(n_steps - 1) % 2])
    for qt in range(n_units // MLA_HEADS):
        for pair in range(MLA_PAIRS):
            halves = []
            for h in (2 * pair, 2 * pair + 1):
                acc = acc_scr[qt * MLA_HEADS + h]
                halves.append(acc[:, :LANES] / acc[:, LANES:])
            o_ref[pair, qt * tq:(qt + 1) * tq, :] = (
                jnp.where(lane < MLA_V, halves[0], halves[1]).astype(BF16))


def _mla(q, k, v, tq=512, q_tiles=1, tk=2048, unroll=14):
    bsz, nh, s, w = q.shape
    rows = tq * q_tiles
    return pl.pallas_call(
        functools.partial(_mla_kernel, tk=tk, unroll=unroll),
        grid=(bsz, s // rows),
        in_specs=[
            pl.BlockSpec((None, nh, rows, w), lambda b, i: (b, 0, i, 0)),
            pl.BlockSpec((None, nh, s, w), lambda b, i: (b, 0, 0, 0)),
            pl.BlockSpec((None, v.shape[1], s, w), lambda b, i: (b, 0, 0, 0)),
        ],
        out_specs=pl.BlockSpec((None, v.shape[1], rows, w), lambda b, i: (b, 0, i, 0)),
        out_shape=jax.ShapeDtypeStruct(v.shape, BF16),
        scratch_shapes=[pltpu.VMEM((tq, tk), F32), pltpu.VMEM((tq, tk), F32),
                        pltpu.VMEM((tq, tk), BF16), pltpu.VMEM((tq, tk), BF16),
                        pltpu.VMEM((q_tiles * nh, tq, 2 * LANES), F32)],
        compiler_params=pltpu.CompilerParams(
            dimension_semantics=("arbitrary", "arbitrary"), vmem_limit_bytes=VMEM_LIMIT),
        name="mla",
    )(q, k, v)


def _dil_kernel(q_ref, k_ref, v_ref, o_ref, lse_ref, *, n_side, sub):
    n_res, chunk, _ = q_ref.shape
    seq = k_ref.shape[1]
    win = min(seq, sub + 2 * n_side)
    base = 0 if chunk == seq else pl.program_id(2) * chunk
    rel = (lax.broadcasted_iota(jnp.int32, (sub, win), 1)
           - lax.broadcasted_iota(jnp.int32, (sub, win), 0))
    lane_head = lax.broadcasted_iota(jnp.int32, (1, DIL_GROUP_W), 1) // DIL_HEAD_DIM
    head_masks = [lane_head == h for h in range(DIL_HEADS)]

    def scores(r, t):
        qs = base + t * sub
        if isinstance(qs, int):
            ks = min(max(qs - n_side, 0), seq - win)
        else:
            ks = pl.multiple_of(jnp.clip(qs - n_side, 0, seq - win), n_side)
        q = q_ref[r, t * sub:(t + 1) * sub, :]
        kw = k_ref[r, pl.ds(ks, win), :]
        valid = jnp.abs(rel + (ks - qs)) <= n_side
        s = [jnp.where(valid, _dot_nt(jnp.where(hm, q, jnp.zeros_like(q)), kw), NEG_INF)
             for hm in head_masks]
        return r, t, ks, s

    def finish(r, t, ks, s):
        vw = v_ref[r, pl.ds(ks, win), :]
        m = [jnp.max(sh, axis=-1, keepdims=True) for sh in s]
        p = [jnp.exp2(sh - mh) for sh, mh in zip(s, m)]
        l = [jnp.sum(ph, axis=-1, keepdims=True) for ph in p]
        pv = [_dot(ph.astype(BF16), vw) for ph in p]
        o = jnp.zeros((sub, DIL_GROUP_W), F32)
        lse = jnp.zeros((sub, DIL_GROUP_W), F32)
        for hm, mh, lh, pvh in zip(head_masks, m, l, pv):
            o = jnp.where(hm, pvh / lh, o)
            lse = jnp.where(hm, mh * LN2 + jnp.log(lh), lse)
        o_ref[r, t * sub:(t + 1) * sub, :] = o.astype(BF16)
        lse_ref[r, t * sub:(t + 1) * sub, :] = lse

    tiles = [(r, t) for r in range(n_res) for t in range(chunk // sub)]
    pending = scores(*tiles[0])
    for nxt in tiles[1:]:
        upcoming = scores(*nxt)
        finish(*pending)
        pending = upcoming
    finish(*pending)


def _dilated(qkv, n_side, n_res, chunk):
    bsz, dil, seq, _ = qkv.shape
    sub = seq if seq <= 4 * n_side else 2 * n_side
    w = DIL_GROUP_W
    qspec = pl.BlockSpec((None, n_res, chunk, w), lambda b, r, i: (b, r, i, 0))
    kspec = pl.BlockSpec((None, n_res, seq, w), lambda b, r, i: (b, r, 0, 1))
    vspec = pl.BlockSpec((None, n_res, seq, w), lambda b, r, i: (b, r, 0, 2))
    out = (bsz, dil, seq, w)
    return pl.pallas_call(
        functools.partial(_dil_kernel, n_side=n_side, sub=sub),
        grid=(bsz, dil // n_res, seq // chunk),
        in_specs=[qspec, kspec, vspec],
        out_specs=(qspec, qspec),
        out_shape=(jax.ShapeDtypeStruct(out, BF16), jax.ShapeDtypeStruct(out, F32)),
        compiler_params=pltpu.CompilerParams(
            dimension_semantics=("arbitrary",) * 3, vmem_limit_bytes=VMEM_LIMIT),
        name="dilated",
    )(qkv, qkv, qkv)


def _post_kernel(x_ref, shift_ref, scale_ref, gate_ref, gpre_ref, gpost_ref, wgate_ref,
                 omla_ref, o0_ref, o1_ref, o2_ref, l0_ref, l1_ref, l2_ref,
                 womla_ref, wodil_ref, wout_ref, out_ref, merge_scr):
    tm = x_ref.shape[0]
    rows_per = tm // POST_ROW_GROUPS
    halves = DIL_GROUP_W // LANES

    def natural(ref, part, slot):
        dil = ref.shape[0]
        n = rows_per // dil
        if dil == 1:
            return ref[0, part * n:(part + 1) * n, :].astype(F32)
        for r in range(dil):
            blk = ref[r, part * n:(part + 1) * n, :].astype(F32)
            for hf in range(halves):
                merge_scr[slot * halves + hf, pl.ds(part * rows_per + r, n, stride=dil), :] = (
                    blk[:, hf * LANES:(hf + 1) * LANES])
        rows = slice(part * rows_per, (part + 1) * rows_per)
        return jnp.concatenate([merge_scr[slot * halves + hf, rows, :] for hf in range(halves)],
                               axis=-1)

    def mixed_dilated(part):
        l0, l1, l2 = natural(l0_ref, part, 0), natural(l1_ref, part, 1), natural(l2_ref, part, 2)
        mx = jnp.maximum(jnp.maximum(l0, l1), l2)
        e0, e1, e2 = jnp.exp(l0 - mx), jnp.exp(l1 - mx), jnp.exp(l2 - mx)
        inv = 1.0 / (e0 + e1 + e2)
        return ((e0 * inv) * natural(o0_ref, part, 3) + (e1 * inv) * natural(o1_ref, part, 4)
                + (e2 * inv) * natural(o2_ref, part, 5)).astype(BF16)

    parts = [slice(p * rows_per, (p + 1) * rows_per) for p in range(POST_ROW_GROUPS)]
    us = [(_rms(x_ref[rows, :], gpre_ref[...]) * (1.0 + scale_ref[...])
           + shift_ref[...]).astype(BF16) for rows in parts]

    def first_matmuls(part):
        rows = parts[part]
        o_mla = jnp.concatenate([omla_ref[p, rows, :] for p in range(MLA_PAIRS)], axis=-1)
        return (_dot(us[part], wgate_ref[:, :D_MODEL]), _dot(us[part], wgate_ref[:, D_MODEL:]),
                _dot(o_mla, womla_ref[...]))

    def branch_matmuls(part):
        return first_matmuls(part) + (_dot(mixed_dilated(part), wodil_ref[...]),)

    def merged(ga, gb, ma, md):
        return (jax.nn.sigmoid(ga) * ma + jax.nn.sigmoid(gb) * md).astype(BF16)

    def finish(part, y):
        rows = parts[part]
        out_ref[rows, :] = x_ref[rows, :] + gate_ref[...] * _rms(y, gpost_ref[...])

    branches = [branch_matmuls(part) for part in range(POST_ROW_GROUPS)]
    ys = [_dot(merged(*br), wout_ref[...]) for br in branches]
    for part, y in enumerate(ys):
        finish(part, y)


def _post(x, shift, scale, gate, g_pre, g_post, w_gates, o_mla, o_dil, lse_dil,
          w_o_mla, w_o_dil, w_out, tm=1024):
    bsz, s, d = x.shape
    tok = lambda n: pl.BlockSpec((None, tm, n), lambda b, i: (b, i, 0))
    mod = pl.BlockSpec((None, 1, d), lambda b, i: (b, 0, 0))
    gw = DIL_GROUP_W
    split = [pl.BlockSpec((None, dil, tm // dil, gw), lambda b, i: (b, 0, i, 0))
             for _, dil in DIL_PATTERNS]
    return pl.pallas_call(
        _post_kernel,
        grid=(bsz, s // tm),
        in_specs=[tok(d), mod, mod, mod, _const_spec((1, d)), _const_spec((1, d)),
                  _const_spec(w_gates.shape),
                  pl.BlockSpec((None, MLA_PAIRS, tm, LANES), lambda b, i: (b, 0, i, 0)),
                  *split, *split,
                  _const_spec(w_o_mla.shape), _const_spec(w_o_dil.shape),
                  _const_spec(w_out.shape)],
        out_specs=tok(d),
        out_shape=jax.ShapeDtypeStruct(x.shape, F32),
        scratch_shapes=[pltpu.VMEM((2 * N_DIL_GROUPS * gw // LANES, tm, LANES), F32)],
        compiler_params=pltpu.CompilerParams(
            dimension_semantics=("arbitrary", "arbitrary"), vmem_limit_bytes=VMEM_LIMIT),
        name="post",
    )(x, shift, scale, gate, g_pre, g_post, w_gates, o_mla, *o_dil, *lse_dil,
      w_o_mla, w_o_dil, w_out)


def _tables():
    t = np.zeros((N_TABLE_ROWS, LANES), np.float32)
    scale = np.float32((MLA_NOPE + MLA_ROPE) ** -0.5 * LOG2E)
    half = MLA_ROPE // 2
    f_m = MLA_THETA ** (-np.arange(0, MLA_ROPE, 2, dtype=np.float32) / MLA_ROPE)
    f_d = ROPE_THETA ** (-np.arange(0, ROPE_PART, 2, dtype=np.float32) / ROPE_PART)
    t[ROW_INVF, :ROPE_PART] = np.tile(f_d, 2)
    t[ROW_INVF, LANES // 2:] = np.tile(f_m, 4)
    t[ROW_ONE_Q, :MLA_NOPE] = scale
    t[ROW_MC_Q, MLA_NOPE:MLA_NOPE + MLA_ROPE] = scale
    t[ROW_MS_Q, MLA_NOPE + MLA_ROPE:MLA_NOPE + MLA_ROPE + half] = -scale
    t[ROW_MS_Q, MLA_NOPE + MLA_ROPE + half:] = scale
    t[ROW_MC_K, :MLA_ROPE] = 1.0
    t[ROW_MS_K, MLA_ROPE:MLA_ROPE + half] = -1.0
    t[ROW_MS_K, MLA_ROPE + half:2 * MLA_ROPE] = 1.0
    hp = ROPE_PART // 2
    for off in (0, DIL_HEAD_DIM):
        t[ROW_M2_D, off:off + hp] = -1.0
        t[ROW_M1_D, off + hp:off + ROPE_PART] = 1.0
    return jnp.asarray(t)


def _layout_weights(w_in, w_uq, w_ukv):
    o_ckv = MLA_Q_LORA
    o_kr = o_ckv + MLA_KV_LORA
    o_qkv = o_kr + MLA_ROPE
    o_ga = o_qkv + DIL_QKV
    half = MLA_ROPE // 2
    k_raw = w_in[:, o_kr:o_qkv]
    k_swap = jnp.concatenate([k_raw[:, half:], k_raw[:, :half]], axis=1)
    pad = jnp.zeros((D_MODEL, 2 * MLA_Q_LORA - MLA_Q_LORA - MLA_KV_LORA - 2 * MLA_ROPE), w_in.dtype)
    w_in2 = jnp.concatenate(
        [w_in[:, :o_kr], k_raw, k_swap, pad, w_in[:, o_qkv:o_ga]], axis=1).astype(BF16)
    w_gates = w_in[:, o_ga:].astype(BF16)

    wq = w_uq.reshape(MLA_Q_LORA, MLA_HEADS, MLA_NOPE + MLA_ROPE)
    nope, t1, t2 = wq[..., :MLA_NOPE], wq[..., MLA_NOPE:MLA_NOPE + half], wq[..., MLA_NOPE + half:]
    w_uq2 = jnp.concatenate([nope, t1, t2, t2, t1], axis=-1).reshape(
        MLA_Q_LORA, MLA_HEADS * MLA_PAD).astype(BF16)

    wkv = w_ukv.reshape(MLA_KV_LORA, MLA_HEADS, MLA_NOPE + MLA_V)
    k_cols = jnp.concatenate(
        [wkv[..., :MLA_NOPE], jnp.zeros((MLA_KV_LORA, MLA_HEADS, MLA_PAD - MLA_NOPE), w_ukv.dtype)],
        axis=-1).reshape(MLA_KV_LORA, MLA_HEADS * MLA_PAD)
    v_cols = wkv[..., MLA_NOPE:].reshape(MLA_KV_LORA, MLA_HEADS * MLA_V)
    place = np.zeros((MLA_ROPE, MLA_HEADS, MLA_PAD), np.float32)
    for i in range(MLA_ROPE):
        place[i, :, MLA_NOPE + i] = 1.0
        place[i, :, MLA_NOPE + MLA_ROPE + i] = 1.0
    place = jnp.asarray(place.reshape(MLA_ROPE, MLA_HEADS * MLA_PAD))
    n_lhs = 2 * MLA_Q_LORA - MLA_Q_LORA
    k_rows = jnp.concatenate(
        [k_cols, place, place,
         jnp.zeros((n_lhs - MLA_KV_LORA - 2 * MLA_ROPE, MLA_HEADS * MLA_PAD), F32)], axis=0)
    v_rows = jnp.concatenate(
        [v_cols, jnp.zeros((n_lhs - MLA_KV_LORA, MLA_HEADS * MLA_V), F32)], axis=0)
    w_kv = jnp.concatenate([k_rows, v_rows], axis=1).astype(BF16)
    return w_in2, w_gates, w_uq2, w_kv


def _ff_weights(w_gate, w_up, w_down):
    return w_gate.astype(BF16), w_up.astype(BF16), w_down.astype(BF16)


def kernel(x, c, positions, w_ada, b_ada, g_pre_ff1, w_gate1, w_up1, w_down1, g_post_ff1, g_pre_mix, w_in, g_cq, w_uq, g_ckv, w_ukv, w_o_mla, w_o_dil, w_out, g_post_mix, g_pre_ff2, w_gate2, w_up2, w_down2, g_post_ff2):
    bsz, s, d = x.shape
    depth = w_ada.shape[0]
    pos = positions.astype(F32).reshape(bsz, s, 1)
    tables = _tables()
    for l in range(depth):
        ada = _ada(c, w_ada, b_ada, l).reshape(bsz, 3, 3, 1, d)
        mod = lambda i, j: ada[:, i, j]

        x = _ffn(x, mod(0, 0), mod(0, 1), mod(0, 2), g_pre_ff1[l][None], g_post_ff1[l][None],
                 *_ff_weights(w_gate1[l], w_up1[l], w_down1[l]))

        w_in2, w_gates, w_uq2, w_kv = _layout_weights(w_in[l], w_uq[l], w_ukv[l])
        q_m, k_m, v_m, *dil_qkv = _proj(x, pos, mod(1, 0), mod(1, 1), g_pre_mix[l][None], tables,
                                        w_in2, g_cq[l][None], w_uq2, g_ckv[l][None], w_kv)
        o_mla = _mla(q_m, k_m, v_m)

        outs, lses = [], []
        for qkv_g, (window, dil) in zip(dil_qkv, DIL_PATTERNS):
            chunk = min(s // dil, DIL_STEP_QUERIES)
            o_g, lse_g = _dilated(qkv_g, n_side=window // (2 * dil),
                                  n_res=min(dil, DIL_STEP_QUERIES // chunk), chunk=chunk)
            outs.append(o_g)
            lses.append(lse_g)

        x = _post(x, mod(1, 0), mod(1, 1), mod(1, 2), g_pre_mix[l][None], g_post_mix[l][None],
                  w_gates, o_mla, outs, lses,
                  w_o_mla[l].astype(BF16), w_o_dil[l].astype(BF16), w_out[l].astype(BF16))

        x = _ffn(x, mod(2, 0), mod(2, 1), mod(2, 2), g_pre_ff2[l][None], g_post_ff2[l][None],
                 *_ff_weights(w_gate2[l], w_up2[l], w_down2[l]))
    return x
```
